```python
import jax
import jax.numpy as jnp
from jax import lax
import numpy as np

D_MODEL = 1024
BATCH = 16
SEQ = 4096
DEPTH = 2
DEC_BATCH = 32
DEC_SEQ = 16
PAST_LEN = 4096

CHUNK = 64
LEFT_CHUNKS = 8
WINDOW = LEFT_CHUNKS * CHUNK
BAND = WINDOW + CHUNK
ATT_HEADS = 16
ATT_HEAD_DIM = D_MODEL // ATT_HEADS
REL_CLIP = 128
N_REL = REL_CLIP + CHUNK
HG_HEADS = 8
HG_EXPAND = 128
HG_VDIM = D_MODEL // HG_HEADS
HG_FDIM = HG_HEADS * HG_EXPAND
HG_BLOCK = 16
D_FF = 2816
N_EXPERTS = 8
TOP_K = 2
D_FF_EXPERT = 3584
N_ATT_LAYERS = (DEPTH + 1) // 2
N_HG_LAYERS = DEPTH // 2
RMS_EPS = 1e-6

kernel_name = 'hybrid_chunkattn_hgrn2_stream_step'


def rms_norm(x, gain):
    xf = x.astype(jnp.float32)
    y = xf * lax.rsqrt(jnp.mean(xf * xf, axis=-1, keepdims=True) + RMS_EPS)
    return (y * gain.astype(jnp.float32)).astype(x.dtype)


def rel_bias(table, rel):
    idx = jnp.clip(rel, -(CHUNK - 1), REL_CLIP) + (CHUNK - 1)
    return table.astype(jnp.float32)[:, idx]


def split_qkv(h, w_qkv):
    b, t, _ = h.shape
    qkv = (h @ w_qkv).reshape(b, t, 3, ATT_HEADS, ATT_HEAD_DIM)
    return qkv[:, :, 0], qkv[:, :, 1], qkv[:, :, 2]


def band_attention_prompt(h, w_qkv, w_o, table):
    b, t, _ = h.shape
    nc = t // CHUNK
    q, k, v = split_qkv(h, w_qkv)
    pad = ((0, 0), (WINDOW, 0), (0, 0), (0, 0))
    k_pad, v_pad = jnp.pad(k, pad), jnp.pad(v, pad)
    q_chunks = jnp.moveaxis(q.reshape(b, nc, CHUNK, ATT_HEADS, ATT_HEAD_DIM), 1, 0)
    qi = jnp.arange(CHUNK)[:, None]
    kj = jnp.arange(BAND)[None, :]
    bias = rel_bias(table, WINDOW + qi - kj)
    scale = ATT_HEAD_DIM ** -0.5

    def one_chunk(args):
        q_c, c = args
        start = c * CHUNK
        k_c = lax.dynamic_slice_in_dim(k_pad, start, BAND, axis=1)
        v_c = lax.dynamic_slice_in_dim(v_pad, start, BAND, axis=1)
        s = jnp.einsum('bqhd,bkhd->bhqk', q_c, k_c).astype(jnp.float32) * scale + bias
        valid = (start + kj) >= WINDOW
        s = jnp.where(valid, s, jnp.finfo(jnp.float32).min)
        p = jax.nn.softmax(s, axis=-1).astype(v_c.dtype)
        return jnp.einsum('bhqk,bkhd->bqhd', p, v_c)

    o = lax.map(one_chunk, (q_chunks, jnp.arange(nc)))
    o = jnp.moveaxis(o, 0, 1).reshape(b, t, D_MODEL)
    keep = min(WINDOW, t)
    return o @ w_o, k[:, t - keep:], v[:, t - keep:]


def band_attention_sample(h, cache_k, cache_v, w_qkv, w_o, table):
    b, s_len, _ = h.shape
    w = cache_k.shape[1]
    q, k, v = split_qkv(h, w_qkv)
    keys = jnp.concatenate([cache_k.astype(k.dtype), k], axis=1)
    vals = jnp.concatenate([cache_v.astype(v.dtype), v], axis=1)
    qi = jnp.arange(s_len)[:, None]
    kj = jnp.arange(w + s_len)[None, :]
    bias = rel_bias(table, w + qi - kj)
    s = jnp.einsum('bqhd,bkhd->bhqk', q, keys).astype(jnp.float32) * (ATT_HEAD_DIM ** -0.5) + bias
    p = jax.nn.softmax(s, axis=-1).astype(vals.dtype)
    o = jnp.einsum('bhqk,bkhd->bqhd', p, vals).reshape(b, s_len, D_MODEL)
    return o @ w_o, k, v


def gla_scan(q, k, v, log_f, s0):
    b, t, nh, _ = q.shape
    dv = v.shape[-1]
    pad = (-t) % HG_BLOCK
    if pad:
        cfg = ((0, 0), (0, pad), (0, 0), (0, 0))
        q, k, v, log_f = (jnp.pad(a, cfg) for a in (q, k, v, log_f))
    n = (t + pad) // HG_BLOCK

    def blocks(a):
        return a.reshape(b, n, HG_BLOCK, nh, a.shape[-1])

    q, k, v, log_f = blocks(q), blocks(k), blocks(v), blocks(log_f)
    g = jnp.cumsum(log_f, axis=2)
    g_last = g[:, :, -1:]
    q_dec = q * jnp.exp(g)
    k_inv = k * jnp.exp(-g)
    k_end = k * jnp.exp(g_last - g)
    causal = jnp.tril(jnp.ones((HG_BLOCK, HG_BLOCK), dtype=bool))
    a = jnp.einsum('bnthk,bnshk->bnhts', q_dec, k_inv)
    a = jnp.where(causal, a, 0.0)
    o_intra = jnp.einsum('bnhts,bnshv->bnthv', a, v)
    decay = jnp.exp(g_last[:, :, 0])

    def step(state, xs):
        q_b, k_b, v_b, d_b = xs
        o_b = jnp.einsum('bthk,bhkv->bthv', q_b, state)
        state = state * d_b[..., None] + jnp.einsum('bthk,bthv->bhkv', k_b, v_b)
        return state, o_b

    xs = tuple(jnp.moveaxis(a_, 1, 0) for a_ in (q_dec, k_end, v, decay))
    s_final, o_inter = lax.scan(step, s0, xs)
    o = o_intra + jnp.moveaxis(o_inter, 0, 1)
    return o.reshape(b, n * HG_BLOCK, nh, dv)[:, :t], s_final


def hgrn2_mixer(h, s0, w_in, w_out, out_norm, lower_bound):
    b, t, _ = h.shape
    proj = (h @ w_in).astype(jnp.float32)
    q, f, i, g = jnp.split(proj, [HG_FDIM, 2 * HG_FDIM, 2 * HG_FDIM + D_MODEL], axis=-1)
    forget = lower_bound + (1.0 - lower_bound) * jax.nn.sigmoid(f)
    q = q.reshape(b, t, HG_HEADS, HG_EXPAND) * (HG_EXPAND ** -0.5)
    k = (1.0 - forget).reshape(b, t, HG_HEADS, HG_EXPAND)
    log_f = jnp.log(forget).reshape(b, t, HG_HEADS, HG_EXPAND)
    v = i.reshape(b, t, HG_HEADS, HG_VDIM)
    o, s_final = gla_scan(q, k, v, log_f, s0.astype(jnp.float32))
    o = o * lax.rsqrt(jnp.mean(o * o, axis=-1, keepdims=True) + RMS_EPS)
    o = o * out_norm.astype(jnp.float32).reshape(HG_HEADS, HG_VDIM)
    o = o.reshape(b, t, D_MODEL) * jax.nn.silu(g)
    return o.astype(h.dtype) @ w_out, s_final.astype(s0.dtype)


def swiglu(h, w_gu, w_down):
    gate, up = jnp.split(h @ w_gu, 2, axis=-1)
    return (jax.nn.silu(gate) * up) @ w_down


def moe_swiglu(h, w_router, w_gu, w_down):
    logits = (h @ w_router).astype(jnp.float32)
    top_v, top_i = lax.top_k(logits, TOP_K)
    wts = jax.nn.softmax(top_v, axis=-1)
    gates = jnp.sum(jax.nn.one_hot(top_i, N_EXPERTS, dtype=jnp.float32) * wts[..., None], axis=-2)
    y = jnp.zeros_like(h)
    for e in range(N_EXPERTS):
        y = y + gates[..., e:e + 1].astype(h.dtype) * swiglu(h, w_gu[e], w_down[e])
    return y


def setup_inputs(seed: int = 0) -> dict:
    key = jax.random.key(seed)
    ks = jax.random.split(key, 20)

    def nrm(k, shape, scale):
        return jax.random.normal(k, shape, jnp.float32) * scale

    cache_rows = min(WINDOW, PAST_LEN)
    return {
        'x_prompt': nrm(ks[0], (BATCH, SEQ, D_MODEL), 1.0),
        'x_sample': nrm(ks[1], (DEC_BATCH, DEC_SEQ, D_MODEL), 1.0),
        'cache_k': nrm(ks[2], (N_ATT_LAYERS, DEC_BATCH, cache_rows, ATT_HEADS, ATT_HEAD_DIM), 1.0),
        'cache_v': nrm(ks[3], (N_ATT_LAYERS, DEC_BATCH, cache_rows, ATT_HEADS, ATT_HEAD_DIM), 1.0),
        'state_hgrn': nrm(ks[4], (N_HG_LAYERS, DEC_BATCH, HG_HEADS, HG_EXPAND, HG_VDIM), 0.5),
        'norm_mix': 1.0 + nrm(ks[5], (DEPTH, D_MODEL), 0.02),
        'norm_ch': 1.0 + nrm(ks[6], (DEPTH, D_MODEL), 0.02),
        'norm_final': 1.0 + nrm(ks[7], (D_MODEL,), 0.02),
        'att_w_qkv': nrm(ks[8], (N_ATT_LAYERS, D_MODEL, 3 * D_MODEL), D_MODEL ** -0.5),
        'att_w_o': nrm(ks[9], (N_ATT_LAYERS, D_MODEL, D_MODEL), D_MODEL ** -0.5),
        'att_rel_bias': nrm(ks[10], (N_ATT_LAYERS, ATT_HEADS, N_REL), 0.1),
        'hg_w_in': nrm(ks[11], (N_HG_LAYERS, D_MODEL, 2 * HG_FDIM + 2 * D_MODEL), D_MODEL ** -0.5),
        'hg_w_out': nrm(ks[12], (N_HG_LAYERS, D_MODEL, D_MODEL), D_MODEL ** -0.5),
        'hg_out_norm': 1.0 + nrm(ks[13], (N_HG_LAYERS, D_MODEL), 0.02),
        'hg_lower_bounds': nrm(ks[14], (DEPTH, HG_FDIM), 0.1),
        'ffn_w_gu': nrm(ks[15], (N_ATT_LAYERS, D_MODEL, 2 * D_FF), D_MODEL ** -0.5),
        'ffn_w_down': nrm(ks[16], (N_ATT_LAYERS, D_FF, D_MODEL), D_FF ** -0.5),
        'moe_w_router': nrm(ks[17], (N_HG_LAYERS, D_MODEL, N_EXPERTS), D_MODEL ** -0.5),
        'moe_w_gu': nrm(ks[18], (N_HG_LAYERS, N_EXPERTS, D_MODEL, 2 * D_FF_EXPERT), D_MODEL ** -0.5),
        'moe_w_down': nrm(ks[19], (N_HG_LAYERS, N_EXPERTS, D_FF_EXPERT, D_MODEL), D_FF_EXPERT ** -0.5),
    }


def reference(x_prompt, x_sample, cache_k, cache_v, state_hgrn,
              norm_mix, norm_ch, norm_final,
              att_w_qkv, att_w_o, att_rel_bias,
              hg_w_in, hg_w_out, hg_out_norm, hg_lower_bounds,
              ffn_w_gu, ffn_w_down,
              moe_w_router, moe_w_gu, moe_w_down):
    lb_all = jnp.cumsum(jax.nn.softmax(hg_lower_bounds.astype(jnp.float32), axis=0), axis=0)
    lb_all = lb_all - lb_all[0]
    xp, xs = x_prompt, x_sample
    k_prompt, v_prompt, k_sample, v_sample = [], [], [], []
    s_prompt, s_sample = [], []
    for layer in range(DEPTH):
        j = layer // 2
        hp = rms_norm(xp, norm_mix[layer])
        hs = rms_norm(xs, norm_mix[layer])
        if layer % 2 == 0:
            mp, kp, vp = band_attention_prompt(hp, att_w_qkv[j], att_w_o[j], att_rel_bias[j])
            ms, kn, vn = band_attention_sample(hs, cache_k[j], cache_v[j],
                                               att_w_qkv[j], att_w_o[j], att_rel_bias[j])
            k_prompt.append(kp)
            v_prompt.append(vp)
            k_sample.append(kn)
            v_sample.append(vn)
        else:
            s0 = jnp.zeros((xp.shape[0], HG_HEADS, HG_EXPAND, HG_VDIM), xp.dtype)
            mp, sp = hgrn2_mixer(hp, s0, hg_w_in[j], hg_w_out[j], hg_out_norm[j], lb_all[layer])
            ms, sn = hgrn2_mixer(hs, state_hgrn[j], hg_w_in[j], hg_w_out[j], hg_out_norm[j], lb_all[layer])
            s_prompt.append(sp)
            s_sample.append(sn)
        xp = xp + mp
        xs = xs + ms
        hp = rms_norm(xp, norm_ch[layer])
        hs = rms_norm(xs, norm_ch[layer])
        if layer % 2 == 0:
            xp = xp + swiglu(hp, ffn_w_gu[j], ffn_w_down[j])
            xs = xs + swiglu(hs, ffn_w_gu[j], ffn_w_down[j])
        else:
            xp = xp + moe_swiglu(hp, moe_w_router[j], moe_w_gu[j], moe_w_down[j])
            xs = xs + moe_swiglu(hs, moe_w_router[j], moe_w_gu[j], moe_w_down[j])
    y_prompt = rms_norm(xp, norm_final)
    y_sample = rms_norm(xs, norm_final)
    return (y_prompt, y_sample,
            jnp.stack(k_prompt), jnp.stack(v_prompt),
            jnp.stack(k_sample), jnp.stack(v_sample),
            jnp.stack(s_prompt), jnp.stack(s_sample))
```

```python
import functools

import jax
import jax.numpy as jnp
import numpy as np
from jax import lax
from jax.experimental import pallas as pl
from jax.experimental.pallas import tpu as pltpu

F32 = jnp.float32
BF16 = jnp.bfloat16
I32 = jnp.int32

D_MODEL = 1024
ATT_HEADS = 16
ATT_HEAD_DIM = 64
CHUNK = 64
LEFT_CHUNKS = 8
WINDOW = LEFT_CHUNKS * CHUNK
REL_CLIP = 128
HG_HEADS = 8
HG_EXPAND = 128
N_EXPERTS = 8
RMS_EPS = 1e-6

LANES = 128
BF16_SUBLANES = 16
VMEM_LIMIT = 56 * 1024 * 1024

HEAD_PAIRS = ATT_HEADS // 2
Q_GROUP = WINDOW
Q_HALF = Q_GROUP // 2
K_SPAN = Q_HALF + WINDOW
MASKED = -1e30

ROUTE_TILE = 512
ROUTE_ALIGN = BF16_SUBLANES
ROUTE_ROWS = 2 * ROUTE_TILE + N_EXPERTS * (ROUTE_ALIGN - 1)
ROUTE_ROWS = -(-ROUTE_ROWS // LANES) * LANES
RUN_SIZES = tuple(ROUTE_TILE >> s for s in range(6))
EXPERT_TILE = 512

_NT = (((1,), (1,)), ((), ()))
_TN = (((0,), (0,)), ((), ()))


def _params(*sem):
    return pltpu.CompilerParams(dimension_semantics=sem, vmem_limit_bytes=VMEM_LIMIT)


def _rms(x, gain):
    return x * lax.rsqrt(jnp.mean(x * x, axis=-1, keepdims=True) + RMS_EPS) * gain


def _sigmoid(x):
    return 1.0 / (1.0 + jnp.exp(-x))


def _const_spec(shape):
    zeros = (0,) * len(shape)
    return pl.BlockSpec(shape, lambda *_: zeros)


def _qkv_kernel(x_ref, g_ref, w_ref, q_ref, k_ref, v_ref, kf_ref, vf_ref, *, tiles_per_seq):
    h = _rms(x_ref[...], g_ref[...]).astype(BF16)
    d = D_MODEL
    q = jnp.dot(h, w_ref[:, 0:d], preferred_element_type=F32) * (ATT_HEAD_DIM ** -0.5)
    k = jnp.dot(h, w_ref[:, d:2 * d], preferred_element_type=F32)
    v = jnp.dot(h, w_ref[:, 2 * d:3 * d], preferred_element_type=F32)
    for p in range(HEAD_PAIRS):
        sl = slice(p * LANES, (p + 1) * LANES)
        q_ref[p] = q[:, sl].astype(BF16)
        k_ref[p] = k[:, sl].astype(BF16)
        v_ref[p] = v[:, sl].astype(BF16)

    @pl.when(pl.program_id(0) % tiles_per_seq == tiles_per_seq - 1)
    def _():
        kf_ref[...] = k
        vf_ref[...] = v


def _qkv_call(x, gain, w, tiles_per_seq, tm):
    n = x.shape[0]
    n_tiles = n // tm
    n_seq = n_tiles // tiles_per_seq
    slab = jax.ShapeDtypeStruct((HEAD_PAIRS, n, LANES), BF16)
    tail = jax.ShapeDtypeStruct((n_seq * tm, D_MODEL), F32)
    slab_spec = pl.BlockSpec((HEAD_PAIRS, tm, LANES), lambda i: (0, i, 0))
    tail_spec = pl.BlockSpec((tm, D_MODEL), lambda i: (i // tiles_per_seq, 0))
    return pl.pallas_call(
        functools.partial(_qkv_kernel, tiles_per_seq=tiles_per_seq),
        grid=(n_tiles,),
        in_specs=[pl.BlockSpec((tm, D_MODEL), lambda i: (i, 0)),
                  _const_spec((1, D_MODEL)),
                  _const_spec((D_MODEL, 3 * D_MODEL))],
        out_specs=[slab_spec, slab_spec, slab_spec, tail_spec, tail_spec],
        out_shape=[slab, slab, slab, tail, tail],
        compiler_params=_params("arbitrary"),
        name="qkv_proj",
    )(x, gain, w)


def _band_attn_kernel(q_ref, kp_ref, kc_ref, vp_ref, vc_ref, bias_ref, o_ref):
    first_group = pl.program_id(2) == 0
    lane = lax.broadcasted_iota(I32, (Q_HALF, LANES), 1)
    low = lane < ATT_HEAD_DIM
    for half in range(2):
        qh = q_ref[half * Q_HALF:(half + 1) * Q_HALF, :]
        if half == 0:
            keys = jnp.concatenate([kp_ref[...], kc_ref[0:Q_HALF, :]], axis=0)
            vals = jnp.concatenate([vp_ref[...], vc_ref[0:Q_HALF, :]], axis=0)
        else:
            keys = jnp.concatenate([kp_ref[Q_HALF:, :], kc_ref[...]], axis=0)
            vals = jnp.concatenate([vp_ref[Q_HALF:, :], vc_ref[...]], axis=0)
        variant = jnp.where(first_group, 1 + half, 0)
        outs = []
        for hd in range(2):
            qm = jnp.where(low if hd == 0 else jnp.logical_not(low), qh, jnp.zeros_like(qh))
            s = lax.dot_general(qm, keys, _NT, preferred_element_type=F32)
            s = s + bias_ref[hd, variant]
            m = jnp.max(s, axis=-1, keepdims=True)
            p = jnp.exp(s - m)
            l = jnp.sum(p, axis=-1, keepdims=True)
            o = jnp.dot(p.astype(BF16), vals, preferred_element_type=F32)
            outs.append(o / l)
        o_ref[half * Q_HALF:(half + 1) * Q_HALF, :] = jnp.where(low, outs[0], outs[1]).astype(BF16)


def _band_bias(table):
    qi = np.arange(Q_HALF)[:, None]
    kj = np.arange(K_SPAN)[None, :]
    idx = np.clip(WINDOW + qi - kj, -(CHUNK - 1), REL_CLIP) + (CHUNK - 1)
    cq, ck = qi // CHUNK, kj // CHUNK
    band = (ck >= cq) & (ck <= cq + LEFT_CHUNKS)
    valid = np.stack([band, band & (kj >= WINDOW), band & (kj >= WINDOW - Q_HALF)])
    b = table.astype(F32)[:, idx]
    b = jnp.where(valid[None], b[:, None], MASKED)
    return b.reshape(HEAD_PAIRS, 2, 3, Q_HALF, K_SPAN)


def _band_attn_call(q, k, v, bias, batch, seq):
    groups = seq // Q_GROUP
    n = batch * seq

    def cur(p, b, g):
        return (p, b * groups + g, 0)

    def prev(p, b, g):
        return (p, b * groups + jnp.maximum(g - 1, 0), 0)

    blk = (None, Q_GROUP, LANES)
    return pl.pallas_call(
        _band_attn_kernel,
        grid=(HEAD_PAIRS, batch, groups),
        in_specs=[pl.BlockSpec(blk, cur), pl.BlockSpec(blk, prev), pl.BlockSpec(blk, cur),
                  pl.BlockSpec(blk, prev), pl.BlockSpec(blk, cur),
                  pl.BlockSpec((None, 2, 3, Q_HALF, K_SPAN), lambda p, b, g: (p, 0, 0, 0, 0))],
        out_specs=pl.BlockSpec((Q_GROUP, LANES), lambda p, b, g: (b * groups + g, p)),
        out_shape=jax.ShapeDtypeStruct((n, D_MODEL), BF16),
        compiler_params=_params("arbitrary", "arbitrary", "arbitrary"),
        name="band_attention",
    )(q, k, k, v, v, bias)


def _cache_attn_kernel(q_ref, k_ref, v_ref, ck_ref, cv_ref, bias_ref, o_ref):
    s_len = q_ref.shape[1]
    lane = lax.broadcasted_iota(I32, (s_len, LANES), 1)
    low = lane < ATT_HEAD_DIM
    for p in range(HEAD_PAIRS):
        sl = slice(p * LANES, (p + 1) * LANES)
        keys = jnp.concatenate([ck_ref[:, sl].astype(BF16), k_ref[p]], axis=0)
        vals = jnp.concatenate([cv_ref[:, sl].astype(BF16), v_ref[p]], axis=0)
        qp = q_ref[p]
        outs = []
        for hd in range(2):
            qm = jnp.where(low if hd == 0 else jnp.logical_not(low), qp, jnp.zeros_like(qp))
            s = lax.dot_general(qm, keys, _NT, preferred_element_type=F32) + bias_ref[2 * p + hd]
            m = jnp.max(s, axis=-1, keepdims=True)
            e = jnp.exp(s - m)
            l = jnp.sum(e, axis=-1, keepdims=True)
            outs.append(jnp.dot(e.astype(BF16), vals, preferred_element_type=F32) / l)
        o_ref[:, sl] = jnp.where(low, outs[0], outs[1]).astype(BF16)


def _cache_bias(table, w, s_len):
    qi = np.arange(s_len)[:, None]
    kj = np.arange(w + s_len)[None, :]
    idx = np.clip(w + qi - kj, -(CHUNK - 1), REL_CLIP) + (CHUNK - 1)
    return table.astype(F32)[:, idx]


def _cache_attn_call(q, k, v, cache_k, cache_v, bias, batch, s_len):
    w = cache_k.shape[1]
    slab = pl.BlockSpec((HEAD_PAIRS, s_len, LANES), lambda b: (0, b, 0))
    cache = pl.BlockSpec((None, w, D_MODEL), lambda b: (b, 0, 0))
    return pl.pallas_call(
        _cache_attn_kernel,
        grid=(batch,),
        in_specs=[slab, slab, slab, cache, cache, _const_spec((ATT_HEADS, s_len, w + s_len))],
        out_specs=pl.BlockSpec((s_len, D_MODEL), lambda b: (b, 0)),
        out_shape=jax.ShapeDtypeStruct((batch * s_len, D_MODEL), BF16),
        compiler_params=_params("arbitrary"),
        name="cache_attention",
    )(q, k, v, cache_k, cache_v, bias)


def _proj_residual_kernel(a_ref, w_ref, x_ref, o_ref):
    o_ref[...] = x_ref[...] + jnp.dot(a_ref[...], w_ref[...], preferred_element_type=F32)


def _proj_residual_call(a, w, x, tm):
    n = x.shape[0]
    row = pl.BlockSpec((tm, D_MODEL), lambda i: (i, 0))
    return pl.pallas_call(
        _proj_residual_kernel,
        grid=(n // tm,),
        in_specs=[row, _const_spec((D_MODEL, D_MODEL)), row],
        out_specs=row,
        out_shape=jax.ShapeDtypeStruct((n, D_MODEL), F32),
        compiler_params=_params("arbitrary"),
        name="proj_residual",
    )(a, w, x)


def _ffn_kernel(x_ref, g_ref, wg_ref, wu_ref, wd_ref, o_ref, h_ref, acc_ref):
    j = pl.program_id(1)

    @pl.when(j == 0)
    def _():
        h_ref[...] = _rms(x_ref[...], g_ref[...]).astype(BF16)
        acc_ref[...] = jnp.zeros_like(acc_ref)

    h = h_ref[...]
    gate = jnp.dot(h, wg_ref[...], preferred_element_type=F32)
    up = jnp.dot(h, wu_ref[...], preferred_element_type=F32)
    act = (gate * _sigmoid(gate) * up).astype(BF16)
    acc_ref[...] += jnp.dot(act, wd_ref[...], preferred_element_type=F32)

    @pl.when(j == pl.num_programs(1) - 1)
    def _():
        o_ref[...] = x_ref[...] + acc_ref[...]


def _ffn_call(x, gain, w_gu, w_down, tm, fc):
    n = x.shape[0]
    d_ff = w_down.shape[0]
    nf = d_ff // fc
    row = pl.BlockSpec((tm, D_MODEL), lambda i, j: (i, 0))
    return pl.pallas_call(
        _ffn_kernel,
        grid=(n // tm, nf),
        in_specs=[row, _const_spec((1, D_MODEL)),
                  pl.BlockSpec((D_MODEL, fc), lambda i, j: (0, j)),
                  pl.BlockSpec((D_MODEL, fc), lambda i, j: (0, j + nf)),
                  pl.BlockSpec((fc, D_MODEL), lambda i, j: (j, 0))],
        out_specs=row,
        out_shape=jax.ShapeDtypeStruct((n, D_MODEL), F32),
        scratch_shapes=[pltpu.VMEM((tm, D_MODEL), BF16), pltpu.VMEM((tm, D_MODEL), F32)],
        compiler_params=_params("arbitrary", "arbitrary"),
        name="swiglu_ffn",
    )(x, gain, w_gu, w_gu, w_down)


def _hg_in_kernel(x_ref, g_ref, w_ref, q_ref, f_ref, i_ref, z_ref):
    h = _rms(x_ref[...], g_ref[...]).astype(BF16)
    d = D_MODEL
    for idx, ref in enumerate((q_ref, f_ref, i_ref, z_ref)):
        ref[...] = jnp.dot(h, w_ref[:, idx * d:(idx + 1) * d], preferred_element_type=F32)


def _hg_in_call(x, gain, w, tm):
    n = x.shape[0]
    row = pl.BlockSpec((tm, D_MODEL), lambda i: (i, 0))
    out = jax.ShapeDtypeStruct((n, D_MODEL), F32)
    return pl.pallas_call(
        _hg_in_kernel,
        grid=(n // tm,),
        in_specs=[row, _const_spec((1, D_MODEL)), _const_spec((D_MODEL, 4 * D_MODEL))],
        out_specs=[row, row, row, row],
        out_shape=[out, out, out, out],
        compiler_params=_params("arbitrary"),
        name="hgrn_in_proj",
    )(x, gain, w)


def _hg_scan_kernel(*refs, chunk, sub, has_s0):
    if has_s0:
        q_ref, f_ref, i_ref, z_ref, lb_ref, on_ref, s0_ref, y_ref, sf_ref, st_ref = refs
    else:
        q_ref, f_ref, i_ref, z_ref, lb_ref, on_ref, y_ref, sf_ref, st_ref = refs
    t = pl.program_id(1)

    @pl.when(t == 0)
    def _():
        if has_s0:
            st_ref[...] = s0_ref[...]
        else:
            st_ref[...] = jnp.zeros_like(st_ref)

    lb = lb_ref[...]
    forget = lb + (1.0 - lb) * _sigmoid(f_ref[...])
    log_f = jnp.log(forget)
    key_all = 1.0 - forget
    r = lax.broadcasted_iota(I32, (chunk, chunk), 0)
    c = lax.broadcasted_iota(I32, (chunk, chunk), 1)
    tri = (c <= r).astype(F32)
    g_all = jnp.dot(tri, log_f, precision=lax.Precision.HIGHEST, preferred_element_type=F32)

    for hd in range(HG_HEADS):
        sl = slice(hd * HG_EXPAND, (hd + 1) * HG_EXPAND)
        g = g_all[:, sl]
        q = q_ref[:, sl] * (HG_EXPAND ** -0.5)
        k = key_all[:, sl]
        v = i_ref[:, sl].astype(BF16)
        g_last = g[chunk - 1:chunk, :]
        state = st_ref[hd]
        q_in = (q * jnp.exp(g)).astype(BF16)
        k_end = (k * jnp.exp(g_last - g)).astype(BF16)
        o_inter = lax.dot_general(q_in, state.astype(BF16), _NT, preferred_element_type=F32)
        new_state = state * jnp.exp(g_last) + lax.dot_general(v, k_end, _TN, preferred_element_type=F32)
        st_ref[hd] = new_state

        @pl.when(t == pl.num_programs(1) - 1)
        def _():
            sf_ref[hd] = new_state

        rows = []
        for r0 in range(0, chunk, sub):
            nk = r0 + sub
            g_mid = g[r0 + sub // 2 - 1:r0 + sub // 2, :]
            q_blk = (q[r0:nk] * jnp.exp(g[r0:nk] - g_mid)).astype(BF16)
            k_blk = (k[:nk] * jnp.exp(g_mid - g[:nk])).astype(BF16)
            a = lax.dot_general(q_blk, k_blk, _NT, preferred_element_type=F32)
            row_pos = r0 + lax.broadcasted_iota(I32, (sub, nk), 0)
            col_pos = lax.broadcasted_iota(I32, (sub, nk), 1)
            a = jnp.where(col_pos <= row_pos, a, 0.0).astype(BF16)
            rows.append(jnp.dot(a, v[:nk], preferred_element_type=F32))
        o = o_inter + (rows[0] if len(rows) == 1 else jnp.concatenate(rows, axis=0))
        o = o * lax.rsqrt(jnp.mean(o * o, axis=-1, keepdims=True) + RMS_EPS) * on_ref[:, sl]
        z = z_ref[:, sl]
        y_ref[:, sl] = (o * (z * _sigmoid(z))).astype(BF16)


def _hg_scan_call(q, f, i, z, lb, out_norm, s0_t, batch, seq, chunk, sub):
    n_chunks = seq // chunk
    has_s0 = s0_t is not None
    row = pl.BlockSpec((chunk, D_MODEL), lambda b, t: (b * n_chunks + t, 0))
    st = pl.BlockSpec((None, HG_HEADS, HG_EXPAND, HG_EXPAND), lambda b, t: (b, 0, 0, 0))
    in_specs = [row, row, row, row, _const_spec((1, D_MODEL)), _const_spec((1, D_MODEL))]
    args = [q, f, i, z, lb, out_norm]
    if has_s0:
        in_specs.append(st)
        args.append(s0_t)
    return pl.pallas_call(
        functools.partial(_hg_scan_kernel, chunk=chunk, sub=sub, has_s0=has_s0),
        grid=(batch, n_chunks),
        in_specs=in_specs,
        out_specs=[row, st],
        out_shape=[jax.ShapeDtypeStruct((batch * seq, D_MODEL), BF16),
                   jax.ShapeDtypeStruct((batch, HG_HEADS, HG_EXPAND, HG_EXPAND), F32)],
        scratch_shapes=[pltpu.VMEM((HG_HEADS, HG_EXPAND, HG_EXPAND), F32)],
        compiler_params=_params("arbitrary", "arbitrary"),
        name="hgrn_scan",
    )(*args)


def _router_kernel(x_ref, g_ref, wr_ref, h_ref, meta_ref, cnt_ref):
    tr = x_ref.shape[0]
    hf = _rms(x_ref[...], g_ref[...])
    h_ref[...] = hf.astype(BF16)
    logits = jnp.dot(hf, wr_ref[...], precision=lax.Precision.HIGHEST, preferred_element_type=F32)
    lane = lax.broadcasted_iota(I32, (tr, LANES), 1).astype(F32)
    lg = jnp.where(lane < N_EXPERTS, logits, -jnp.inf)
    m1 = jnp.max(lg, axis=-1, keepdims=True)
    i1 = jnp.min(jnp.where(lg == m1, lane, float(LANES)), axis=-1, keepdims=True)
    lg2 = jnp.where(lane == i1, -jnp.inf, lg)
    m2 = jnp.max(lg2, axis=-1, keepdims=True)
    i2 = jnp.min(jnp.where(lg2 == m2, lane, float(LANES)), axis=-1, keepdims=True)
    e = jnp.exp(m2 - m1)
    gate1 = 1.0 / (1.0 + e)
    gate2 = e / (1.0 + e)

    chosen = jnp.logical_or(lane == i1, lane == i2)
    r = lax.broadcasted_iota(I32, (tr, tr), 0)
    c = lax.broadcasted_iota(I32, (tr, tr), 1)
    before = (c < r).astype(BF16)
    rank = jnp.dot(before, chosen.astype(BF16), preferred_element_type=F32)
    counts = jnp.sum(chosen.astype(F32), axis=0, keepdims=True)
    padded = jnp.floor((counts + (ROUTE_ALIGN - 1)) * (1.0 / ROUTE_ALIGN)) * ROUTE_ALIGN
    r2 = lax.broadcasted_iota(I32, (LANES, LANES), 0)
    c2 = lax.broadcasted_iota(I32, (LANES, LANES), 1)
    run_start = jnp.dot(jnp.broadcast_to(padded, (8, LANES)).astype(BF16), (r2 < c2).astype(BF16),
                        preferred_element_type=F32)[0:1]
    slot = run_start + rank
    slot1 = jnp.sum(jnp.where(lane == i1, slot, 0.0), axis=-1, keepdims=True)
    slot2 = jnp.sum(jnp.where(lane == i2, slot, 0.0), axis=-1, keepdims=True)
    meta = jnp.where(lane == 0, slot1, 0.0)
    for idx, val in ((1, slot2), (2, gate1), (3, gate2), (4, i1), (5, i2)):
        meta = jnp.where(lane == idx, val, meta)
    meta_ref[...] = meta
    cnt_ref[...] = jnp.broadcast_to(counts.astype(I32)[None], cnt_ref.shape)


def _router_call(x, gain, w_router_padded):
    n = x.shape[0]
    n_tiles = n // ROUTE_TILE
    row = pl.BlockSpec((ROUTE_TILE, D_MODEL), lambda i: (i, 0))
    return pl.pallas_call(
        _router_kernel,
        grid=(n_tiles,),
        in_specs=[row, _const_spec((1, D_MODEL)), _const_spec((D_MODEL, LANES))],
        out_specs=[row, pl.BlockSpec((ROUTE_TILE, LANES), lambda i: (i, 0)),
                   pl.BlockSpec((1, 8, LANES), lambda i: (i, 0, 0))],
        out_shape=[jax.ShapeDtypeStruct((n, D_MODEL), BF16),
                   jax.ShapeDtypeStruct((n, LANES), F32),
                   jax.ShapeDtypeStruct((n_tiles, 8, LANES), I32)],
        compiler_params=_params("arbitrary"),
        name="moe_router",
    )(x, gain, w_router_padded)


def _run_copies(tbl_ref, tile, stage_ref, hbm_ref, sem, to_hbm):
    base = tile * (3 * N_EXPERTS)
    for e in range(N_EXPERTS):
        lo = tbl_ref[base + e]
        n = tbl_ref[base + N_EXPERTS + e]
        dst = tbl_ref[base + 2 * N_EXPERTS + e]
        for sz in RUN_SIZES:
            done = jnp.bitwise_and(n, ~(2 * sz - 1))
            st = stage_ref.at[pl.ds(pl.multiple_of(lo + done, ROUTE_ALIGN), sz)]
            hb = hbm_ref.at[pl.ds(pl.multiple_of(dst + done, ROUTE_ALIGN), sz)]
            copy = pltpu.make_async_copy(st, hb, sem) if to_hbm else pltpu.make_async_copy(hb, st, sem)
            yield jnp.bitwise_and(n, sz) != 0, copy


def _dispatch_kernel(tbl_ref, h_ref, meta_ref, xs_ref, stage_ref, sem):
    tile = pl.program_id(0)
    meta = meta_ref[...]
    slot1 = meta[:, 0:1].astype(I32)
    slot2 = meta[:, 1:2].astype(I32)
    rows = lax.broadcasted_iota(I32, (ROUTE_TILE, ROUTE_ROWS), 1)
    pick = jnp.logical_or(rows == slot1, rows == slot2).astype(BF16)
    stage_ref[...] = lax.dot_general(pick, h_ref[...], _TN, preferred_element_type=F32).astype(BF16)
    for cond, copy in _run_copies(tbl_ref, tile, stage_ref, xs_ref, sem, True):
        pl.when(cond)(copy.start)
    for cond, copy in _run_copies(tbl_ref, tile, stage_ref, xs_ref, sem, True):
        pl.when(cond)(copy.wait)


def _dispatch_call(tbl, h, meta, sorted_rows):
    n = h.shape[0]
    grid_spec = pltpu.PrefetchScalarGridSpec(
        num_scalar_prefetch=1,
        grid=(n // ROUTE_TILE,),
        in_specs=[pl.BlockSpec((ROUTE_TILE, D_MODEL), lambda i, tbl: (i, 0)),
                  pl.BlockSpec((ROUTE_TILE, LANES), lambda i, tbl: (i, 0))],
        out_specs=pl.BlockSpec(memory_space=pl.ANY),
        scratch_shapes=[pltpu.VMEM((ROUTE_ROWS, D_MODEL), BF16), pltpu.SemaphoreType.DMA(())],
    )
    return pl.pallas_call(
        _dispatch_kernel,
        grid_spec=grid_spec,
        out_shape=jax.ShapeDtypeStruct((sorted_rows, D_MODEL), BF16),
        compiler_params=_params("arbitrary"),
        name="moe_dispatch",
    )(tbl, h, meta)


def _expert_kernel(te_ref, trow_ref, nval_ref, x_ref, wg_ref, wu_ref, wd_ref, o_ref, acc_ref):
    i = pl.program_id(0)
    j = pl.program_id(1)
    nv = nval_ref[i]

    @pl.when(nv > 0)
    def _():
        @pl.when(j == 0)
        def _():
            acc_ref[...] = jnp.zeros_like(acc_ref)

        x = x_ref[...]
        row = lax.broadcasted_iota(I32, x.shape, 0)
        x = jnp.where(row < nv, x, jnp.zeros_like(x))
        gate = jnp.dot(x, wg_ref[...], preferred_element_type=F32)
        up = jnp.dot(x, wu_ref[...], preferred_element_type=F32)
        act = (gate * _sigmoid(gate) * up).astype(BF16)
        acc_ref[...] += jnp.dot(act, wd_ref[...], preferred_element_type=F32)

        @pl.when(j == pl.num_programs(1) - 1)
        def _():
            o_ref[...] = acc_ref[...].astype(BF16)


def _expert_call(te, trow, nval, xs, w_gu, w_down, fc):
    n_tiles = te.shape[0]
    d_ff = w_down.shape[1]
    nf = d_ff // fc

    def jj(i, j, nval):
        return jnp.where(nval[i] > 0, j, nf - 1)

    grid_spec = pltpu.PrefetchScalarGridSpec(
        num_scalar_prefetch=3,
        grid=(n_tiles, nf),
        in_specs=[pl.BlockSpec((EXPERT_TILE, D_MODEL), lambda i, j, te, trow, nval: (trow[i], 0)),
                  pl.BlockSpec((None, D_MODEL, fc), lambda i, j, te, trow, nval: (te[i], 0, jj(i, j, nval))),
                  pl.BlockSpec((None, D_MODEL, fc),
                               lambda i, j, te, trow, nval: (te[i], 0, jj(i, j, nval) + nf)),
                  pl.BlockSpec((None, fc, D_MODEL), lambda i, j, te, trow, nval: (te[i], jj(i, j, nval), 0))],
        out_specs=pl.BlockSpec((EXPERT_TILE, D_MODEL), lambda i, j, te, trow, nval: (trow[i], 0)),
        scratch_shapes=[pltpu.VMEM((EXPERT_TILE, D_MODEL), F32)],
    )
    return pl.pallas_call(
        _expert_kernel,
        grid_spec=grid_spec,
        out_shape=jax.ShapeDtypeStruct(xs.shape, BF16),
        compiler_params=_params("arbitrary", "arbitrary"),
        name="moe_experts",
    )(te, trow, nval, xs, w_gu, w_gu, w_down)


def _combine_kernel(tbl_ref, ys_ref, meta_ref, x_ref, g_ref, o_ref, stage_ref, sem):
    tile = pl.program_id(0)

    @pl.when(tile == 0)
    def _():
        stage_ref[...] = jnp.zeros_like(stage_ref)

    for cond, copy in _run_copies(tbl_ref, tile, stage_ref, ys_ref, sem, False):
        pl.when(cond)(copy.start)
    for cond, copy in _run_copies(tbl_ref, tile, stage_ref, ys_ref, sem, False):
        pl.when(cond)(copy.wait)
    meta = meta_ref[...]
    rows = lax.broadcasted_iota(I32, (ROUTE_TILE, ROUTE_ROWS), 1)
    stage = stage_ref[...]
    y1 = jnp.dot((rows == meta[:, 0:1].astype(I32)).astype(BF16), stage, preferred_element_type=F32)
    y2 = jnp.dot((rows == meta[:, 1:2].astype(I32)).astype(BF16), stage, preferred_element_type=F32)
    x = x_ref[...] + (meta[:, 2:3] * y1 + meta[:, 3:4] * y2)
    o_ref[...] = _rms(x, g_ref[...])


def _combine_call(tbl, ys, meta, x, gain):
    n = x.shape[0]
    row = pl.BlockSpec((ROUTE_TILE, D_MODEL), lambda i, tbl: (i, 0))
    grid_spec = pltpu.PrefetchScalarGridSpec(
        num_scalar_prefetch=1,
        grid=(n // ROUTE_TILE,),
        in_specs=[pl.BlockSpec(memory_space=pl.ANY),
                  pl.BlockSpec((ROUTE_TILE, LANES), lambda i, tbl: (i, 0)),
                  row,
                  pl.BlockSpec((1, D_MODEL), lambda i, tbl: (0, 0))],
        out_specs=row,
        scratch_shapes=[pltpu.VMEM((ROUTE_ROWS, D_MODEL), BF16), pltpu.SemaphoreType.DMA(())],
    )
    return pl.pallas_call(
        _combine_kernel,
        grid_spec=grid_spec,
        out_shape=jax.ShapeDtypeStruct((n, D_MODEL), F32),
        compiler_params=_params("arbitrary"),
        name="moe_combine",
    )(tbl, ys, meta, x, gain)


def _routing_tables(counts, n_row_tiles):
    padded = (counts + (ROUTE_ALIGN - 1)) // ROUTE_ALIGN * ROUTE_ALIGN
    stage_off = jnp.cumsum(padded, axis=1) - padded
    total = jnp.sum(padded, axis=0)
    tiles_e = (total + (EXPERT_TILE - 1)) // EXPERT_TILE
    region = jnp.cumsum(tiles_e) - tiles_e
    dst = region[None, :] * EXPERT_TILE + jnp.cumsum(padded, axis=0) - padded
    tbl = jnp.concatenate([stage_off, padded, dst], axis=1).reshape(-1).astype(I32)

    used = jnp.sum(tiles_e)
    ids = jnp.arange(n_row_tiles, dtype=I32)
    ends = jnp.cumsum(tiles_e)
    te = jnp.sum((ids[:, None] >= ends[None, :]).astype(I32), axis=1)
    te = jnp.minimum(te, N_EXPERTS - 1)
    nval = jnp.clip(total[te] - (ids - region[te]) * EXPERT_TILE, 0, EXPERT_TILE)
    valid = ids < used
    nval = jnp.where(valid, nval, 0).astype(I32)
    last = jnp.maximum(used - 1, 0)
    te = jnp.where(valid, te, te[last]).astype(I32)
    trow = jnp.where(valid, ids, n_row_tiles).astype(I32)
    return tbl, te, trow, nval


def _moe_final(x, gain, w_router_padded, w_gu, w_down, final_gain, fc):
    n = x.shape[0]
    n_tiles = n // ROUTE_TILE
    max_rows = 2 * n + n_tiles * N_EXPERTS * (ROUTE_ALIGN - 1)
    n_row_tiles = -(-max_rows // EXPERT_TILE) + N_EXPERTS
    h, meta, cnt = _router_call(x, gain, w_router_padded)
    tbl, te, trow, nval = _routing_tables(cnt[:, 0, :N_EXPERTS], n_row_tiles)
    xs = _dispatch_call(tbl, h, meta, (n_row_tiles + 1) * EXPERT_TILE)
    ys = _expert_call(te, trow, nval, xs, w_gu, w_down, fc)
    return _combine_call(tbl, ys, meta, x, final_gain)


def _stream(x, cache_k, cache_v, s0_t, lb1, weights, tm, chunk, sub):
    (norm_mix, norm_ch, norm_final, w_qkv, w_o, rel_table, w_in, w_out, out_norm,
     w_gu, w_down, w_router, w_egu, w_edown) = weights
    batch, seq, _ = x.shape
    n = batch * seq
    x = x.reshape(n, D_MODEL)
    row = lambda a: a.reshape(1, D_MODEL)

    if cache_k is None:
        q, k, v, kf, vf = _qkv_call(x, row(norm_mix[0]), w_qkv, seq // tm, tm)
        att = _band_attn_call(q, k, v, _band_bias(rel_table), batch, seq)
        keep = min(WINDOW, seq)
        new_k = kf.reshape(batch, keep, ATT_HEADS, ATT_HEAD_DIM)
        new_v = vf.reshape(batch, keep, ATT_HEADS, ATT_HEAD_DIM)
    else:
        q, k, v, kf, vf = _qkv_call(x, row(norm_mix[0]), w_qkv, 1, tm)
        w = cache_k.shape[1]
        att = _cache_attn_call(q, k, v, cache_k.reshape(batch, w, D_MODEL), cache_v.reshape(batch, w, D_MODEL),
                               _cache_bias(rel_table, w, seq), batch, seq)
        new_k = kf.reshape(batch, seq, ATT_HEADS, ATT_HEAD_DIM)
        new_v = vf.reshape(batch, seq, ATT_HEADS, ATT_HEAD_DIM)
    x = _proj_residual_call(att, w_o, x, tm)
    x = _ffn_call(x, row(norm_ch[0]), w_gu, w_down, min(n, 2 * tm), 256)

    q, f, i, z = _hg_in_call(x, row(norm_mix[1]), w_in, min(tm, 256))
    y, s_t = _hg_scan_call(q, f, i, z, lb1, row(out_norm), s0_t, batch, seq, chunk, sub)
    x = _proj_residual_call(y, w_out, x, tm)
    y = _moe_final(x, row(norm_ch[1]), w_router, w_egu, w_edown, row(norm_final), 512)
    return y.reshape(batch, seq, D_MODEL), new_k, new_v, jnp.swapaxes(s_t, -1, -2)


def kernel(x_prompt, x_sample, cache_k, cache_v, state_hgrn, norm_mix, norm_ch, norm_final, att_w_qkv, att_w_o, att_rel_bias, hg_w_in, hg_w_out, hg_out_norm, hg_lower_bounds, ffn_w_gu, ffn_w_down, moe_w_router, moe_w_gu, moe_w_down):
    lb = jnp.cumsum(jax.nn.softmax(hg_lower_bounds.astype(F32), axis=0), axis=0)
    lb1 = (lb[1] - lb[0]).reshape(1, D_MODEL)
    w_router = jnp.pad(moe_w_router[0].astype(F32), ((0, 0), (0, LANES - N_EXPERTS)))
    weights = (norm_mix.astype(F32), norm_ch.astype(F32), norm_final.astype(F32),
               att_w_qkv[0].astype(BF16), att_w_o[0].astype(BF16), att_rel_bias[0],
               hg_w_in[0].astype(BF16), hg_w_out[0].astype(BF16), hg_out_norm[0].astype(F32),
               ffn_w_gu[0].astype(BF16), ffn_w_down[0].astype(BF16),
               w_router, moe_w_gu[0].astype(BF16), moe_w_down[0].astype(BF16))

    yp, kp, vp, sp = _stream(x_prompt, None, None, None, lb1, weights, 512, 128, 32)
    s0_t = jnp.swapaxes(state_hgrn[0].astype(F32), -1, -2)
    dec_tokens = x_sample.shape[0] * x_sample.shape[1]
    ys, ks, vs, ss = _stream(x_sample, cache_k[0], cache_v[0], s0_t, lb1, weights,
                             dec_tokens, x_sample.shape[1], x_sample.shape[1])
    return (yp, ys, kp[None], vp[None], ks[None], vs[None], sp[None], ss[None])
```

```python
import functools

import jax
import jax.numpy as jnp
import numpy as np
from jax import lax
from jax.experimental import pallas as pl
from jax.experimental.pallas import tpu as pltpu

F32 = jnp.float32
BF16 = jnp.bfloat16
I32 = jnp.int32

D_MODEL = 1024
ATT_HEADS = 16
ATT_HEAD_DIM = 64
CHUNK = 64
LEFT_CHUNKS = 8
WINDOW = LEFT_CHUNKS * CHUNK
REL_CLIP = 128
HG_HEADS = 8
HG_EXPAND = 128
N_EXPERTS = 8
RMS_EPS = 1e-6

LANES = 128
BF16_SUBLANES = 16
MXU_COLS = 256
VMEM_LIMIT = 56 * 1024 * 1024

HEAD_PAIRS = ATT_HEADS // 2
Q_GROUP = WINDOW
Q_HALF = Q_GROUP // 2
K_SPAN = Q_HALF + WINDOW
PAIRS_PER_STEP = 2
MASKED = -1e30

ROUTE_TILE = 512
ROUTE_ALIGN = BF16_SUBLANES
ROUTE_ROWS = 2 * ROUTE_TILE + N_EXPERTS * (ROUTE_ALIGN - 1)
ROUTE_ROWS = -(-ROUTE_ROWS // LANES) * LANES
RUN_SIZES = tuple(ROUTE_TILE >> s for s in range(6))
EXPERT_TILE = 1024

_NT = (((1,), (1,)), ((), ()))
_TN = (((0,), (0,)), ((), ()))


def _params(*sem):
    return pltpu.CompilerParams(dimension_semantics=sem, vmem_limit_bytes=VMEM_LIMIT)


def _rms(x, gain):
    return x * lax.rsqrt(jnp.mean(x * x, axis=-1, keepdims=True) + RMS_EPS) * gain


def _sigmoid(x):
    return 1.0 / (1.0 + jnp.exp(-x))


def _const_spec(shape):
    zeros = (0,) * len(shape)
    return pl.BlockSpec(shape, lambda *_: zeros)


def _qkv_kernel(x_ref, g_ref, w_ref, q_ref, k_ref, v_ref, kf_ref, vf_ref, *, tiles_per_seq):
    h = _rms(x_ref[...], g_ref[...]).astype(BF16)
    d = D_MODEL
    q = jnp.dot(h, w_ref[:, 0:d], preferred_element_type=F32) * (ATT_HEAD_DIM ** -0.5)
    k = jnp.dot(h, w_ref[:, d:2 * d], preferred_element_type=F32)
    v = jnp.dot(h, w_ref[:, 2 * d:3 * d], preferred_element_type=F32)
    for p in range(HEAD_PAIRS):
        sl = slice(p * LANES, (p + 1) * LANES)
        q_ref[p] = q[:, sl].astype(BF16)
        k_ref[p] = k[:, sl].astype(BF16)
        v_ref[p] = v[:, sl].astype(BF16)

    @pl.when(pl.program_id(0) % tiles_per_seq == tiles_per_seq - 1)
    def _():
        kf_ref[...] = k
        vf_ref[...] = v


def _qkv_call(x, gain, w, tiles_per_seq, tm):
    n = x.shape[0]
    n_tiles = n // tm
    n_seq = n_tiles // tiles_per_seq
    slab = jax.ShapeDtypeStruct((HEAD_PAIRS, n, LANES), BF16)
    tail = jax.ShapeDtypeStruct((n_seq * tm, D_MODEL), F32)
    slab_spec = pl.BlockSpec((HEAD_PAIRS, tm, LANES), lambda i: (0, i, 0))
    tail_spec = pl.BlockSpec((tm, D_MODEL), lambda i: (i // tiles_per_seq, 0))
    return pl.pallas_call(
        functools.partial(_qkv_kernel, tiles_per_seq=tiles_per_seq),
        grid=(n_tiles,),
        in_specs=[pl.BlockSpec((tm, D_MODEL), lambda i: (i, 0)),
                  _const_spec((1, D_MODEL)),
                  _const_spec((D_MODEL, 3 * D_MODEL))],
        out_specs=[slab_spec, slab_spec, slab_spec, tail_spec, tail_spec],
        out_shape=[slab, slab, slab, tail, tail],
        compiler_params=_params("arbitrary"),
        name="qkv_proj",
    )(x, gain, w)


def _band_attn_kernel(q_ref, kp_ref, kc_ref, vp_ref, vc_ref, bias_ref, o_ref):
    first_group = pl.program_id(2) == 0
    lane = lax.broadcasted_iota(I32, (2 * Q_HALF, LANES), 1)
    row = lax.broadcasted_iota(I32, (2 * Q_HALF, LANES), 0)
    own = (lane < ATT_HEAD_DIM) == (row < Q_HALF)
    first_head = lax.broadcasted_iota(I32, (Q_HALF, LANES), 1) < ATT_HEAD_DIM
    for pr in range(PAIRS_PER_STEP):
        for half in range(2):
            qh = q_ref[pr, half * Q_HALF:(half + 1) * Q_HALF, :]
            q2 = jnp.concatenate([qh, qh], axis=0)
            q2 = jnp.where(own, q2, jnp.zeros_like(q2))
            if half == 0:
                keys = jnp.concatenate([kp_ref[pr], kc_ref[pr, 0:Q_HALF, :]], axis=0)
                vals = jnp.concatenate([vp_ref[pr], vc_ref[pr, 0:Q_HALF, :]], axis=0)
            else:
                keys = jnp.concatenate([kp_ref[pr, Q_HALF:, :], kc_ref[pr]], axis=0)
                vals = jnp.concatenate([vp_ref[pr, Q_HALF:, :], vc_ref[pr]], axis=0)
            variant = jnp.where(first_group, 1 + half, 0)
            s = lax.dot_general(q2, keys, _NT, preferred_element_type=F32) + bias_ref[pr, variant]
            m = jnp.max(s, axis=-1, keepdims=True)
            p = jnp.exp(s - m)
            l = jnp.sum(p, axis=-1, keepdims=True)
            o2 = jnp.dot(p.astype(BF16), vals, preferred_element_type=F32) / l
            o = jnp.where(first_head, o2[:Q_HALF], o2[Q_HALF:])
            o_ref[half * Q_HALF:(half + 1) * Q_HALF, pr * LANES:(pr + 1) * LANES] = o.astype(BF16)


def _band_bias(table):
    qi = np.arange(Q_HALF)[:, None]
    kj = np.arange(K_SPAN)[None, :]
    cq, ck = qi // CHUNK, kj // CHUNK
    band = (ck >= cq) & (ck <= cq + LEFT_CHUNKS)
    valid = np.stack([band, band & (kj >= WINDOW), band & (kj >= WINDOW - Q_HALF)])
    table = table.astype(F32)
    n_rel = table.shape[1]
    span = Q_HALF + K_SPAN - 1
    lo_rep = (K_SPAN - 1 - WINDOW) - (CHUNK - 1)
    hi_rep = span - lo_rep - n_rel
    line = jnp.concatenate([jnp.repeat(table[:, :1], lo_rep, axis=1), table,
                            jnp.repeat(table[:, -1:], hi_rep, axis=1)], axis=1)[:, ::-1]
    flow = jnp.tile(line, (1, Q_HALF + 1))[:, :Q_HALF * (span + 1)].reshape(-1, Q_HALF, span + 1)
    b = flow[:, ::-1, :K_SPAN]
    b = jnp.where(valid[None], b[:, None], MASKED).reshape(HEAD_PAIRS, 2, 3, Q_HALF, K_SPAN)
    return jnp.swapaxes(b, 1, 2).reshape(HEAD_PAIRS, 3, 2 * Q_HALF, K_SPAN)


def _band_attn_call(q, k, v, bias, batch, seq):
    groups = seq // Q_GROUP
    n = batch * seq

    def cur(p, b, g):
        return (p, b * groups + g, 0)

    def prev(p, b, g):
        return (p, b * groups + jnp.maximum(g - 1, 0), 0)

    blk = (PAIRS_PER_STEP, Q_GROUP, LANES)
    return pl.pallas_call(
        _band_attn_kernel,
        grid=(HEAD_PAIRS // PAIRS_PER_STEP, batch, groups),
        in_specs=[pl.BlockSpec(blk, cur), pl.BlockSpec(blk, prev), pl.BlockSpec(blk, cur),
                  pl.BlockSpec(blk, prev), pl.BlockSpec(blk, cur),
                  pl.BlockSpec((PAIRS_PER_STEP, 3, 2 * Q_HALF, K_SPAN), lambda p, b, g: (p, 0, 0, 0))],
        out_specs=pl.BlockSpec((Q_GROUP, PAIRS_PER_STEP * LANES), lambda p, b, g: (b * groups + g, p)),
        out_shape=jax.ShapeDtypeStruct((n, D_MODEL), BF16),
        compiler_params=_params("arbitrary", "arbitrary", "arbitrary"),
        name="band_attention",
    )(q, k, k, v, v, bias)


def _cache_attn_kernel(q_ref, k_ref, v_ref, ck_ref, cv_ref, bias_ref, o_ref):
    s_len = q_ref.shape[1]
    lane = lax.broadcasted_iota(I32, (s_len, LANES), 1)
    low = lane < ATT_HEAD_DIM
    for p in range(HEAD_PAIRS):
        sl = slice(p * LANES, (p + 1) * LANES)
        keys = jnp.concatenate([ck_ref[:, sl].astype(BF16), k_ref[p]], axis=0)
        vals = jnp.concatenate([cv_ref[:, sl].astype(BF16), v_ref[p]], axis=0)
        qp = q_ref[p]
        outs = []
        for hd in range(2):
            qm = jnp.where(low if hd == 0 else jnp.logical_not(low), qp, jnp.zeros_like(qp))
            s = lax.dot_general(qm, keys, _NT, preferred_element_type=F32) + bias_ref[2 * p + hd]
            m = jnp.max(s, axis=-1, keepdims=True)
            e = jnp.exp(s - m)
            l = jnp.sum(e, axis=-1, keepdims=True)
            outs.append(jnp.dot(e.astype(BF16), vals, preferred_element_type=F32) / l)
        o_ref[:, sl] = jnp.where(low, outs[0], outs[1]).astype(BF16)


def _cache_bias(table, w, s_len):
    qi = np.arange(s_len)[:, None]
    kj = np.arange(w + s_len)[None, :]
    idx = np.clip(w + qi - kj, -(CHUNK - 1), REL_CLIP) + (CHUNK - 1)
    return table.astype(F32)[:, idx]


def _cache_attn_call(q, k, v, cache_k, cache_v, bias, batch, s_len):
    w = cache_k.shape[1]
    slab = pl.BlockSpec((HEAD_PAIRS, s_len, LANES), lambda b: (0, b, 0))
    cache = pl.BlockSpec((None, w, D_MODEL), lambda b: (b, 0, 0))
    return pl.pallas_call(
        _cache_attn_kernel,
        grid=(batch,),
        in_specs=[slab, slab, slab, cache, cache, _const_spec((ATT_HEADS, s_len, w + s_len))],
        out_specs=pl.BlockSpec((s_len, D_MODEL), lambda b: (b, 0)),
        out_shape=jax.ShapeDtypeStruct((batch * s_len, D_MODEL), BF16),
        compiler_params=_params("arbitrary"),
        name="cache_attention",
    )(q, k, v, cache_k, cache_v, bias)


def _proj_residual_kernel(a_ref, w_ref, x_ref, o_ref):
    o_ref[...] = x_ref[...] + jnp.dot(a_ref[...], w_ref[...], preferred_element_type=F32)


def _proj_residual_call(a, w, x, tm):
    n = x.shape[0]
    row = pl.BlockSpec((tm, D_MODEL), lambda i: (i, 0))
    return pl.pallas_call(
        _proj_residual_kernel,
        grid=(n // tm,),
        in_specs=[row, _const_spec((D_MODEL, D_MODEL)), row],
        out_specs=row,
        out_shape=jax.ShapeDtypeStruct((n, D_MODEL), F32),
        compiler_params=_params("arbitrary"),
        name="proj_residual",
    )(a, w, x)


def _swiglu_block(h, wg_ref, wu_ref, wd_ref, act_ref):
    fc = act_ref.shape[1]
    for c0 in range(0, fc, MXU_COLS):
        sl = slice(c0, min(c0 + MXU_COLS, fc))
        gate = jnp.dot(h, wg_ref[:, sl], preferred_element_type=F32)
        up = jnp.dot(h, wu_ref[:, sl], preferred_element_type=F32)
        act_ref[:, sl] = (gate * _sigmoid(gate) * up).astype(BF16)
    return jnp.dot(act_ref[...], wd_ref[...], preferred_element_type=F32)


def _ffn_kernel(x_ref, g_ref, wg_ref, wu_ref, wd_ref, o_ref, h_ref, act_ref, acc_ref):
    j = pl.program_id(1)

    @pl.when(j == 0)
    def _():
        h_ref[...] = _rms(x_ref[...], g_ref[...]).astype(BF16)
        acc_ref[...] = jnp.zeros_like(acc_ref)

    acc_ref[...] += _swiglu_block(h_ref[...], wg_ref, wu_ref, wd_ref, act_ref)

    @pl.when(j == pl.num_programs(1) - 1)
    def _():
        o_ref[...] = x_ref[...] + acc_ref[...]


def _ffn_call(x, gain, w_gu, w_down, tm, fc):
    n = x.shape[0]
    d_ff = w_down.shape[0]
    nf = d_ff // fc
    row = pl.BlockSpec((tm, D_MODEL), lambda i, j: (i, 0))
    return pl.pallas_call(
        _ffn_kernel,
        grid=(n // tm, nf),
        in_specs=[row, _const_spec((1, D_MODEL)),
                  pl.BlockSpec((D_MODEL, fc), lambda i, j: (0, j)),
                  pl.BlockSpec((D_MODEL, fc), lambda i, j: (0, j + nf)),
                  pl.BlockSpec((fc, D_MODEL), lambda i, j: (j, 0))],
        out_specs=row,
        out_shape=jax.ShapeDtypeStruct((n, D_MODEL), F32),
        scratch_shapes=[pltpu.VMEM((tm, D_MODEL), BF16), pltpu.VMEM((tm, fc), BF16),
                        pltpu.VMEM((tm, D_MODEL), F32)],
        compiler_params=_params("arbitrary", "arbitrary"),
        name="swiglu_ffn",
    )(x, gain, w_gu, w_gu, w_down)


def _hg_in_kernel(x_ref, g_ref, w_ref, q_ref, f_ref, i_ref, z_ref):
    h = _rms(x_ref[...], g_ref[...]).astype(BF16)
    d = D_MODEL
    for idx, ref in enumerate((q_ref, f_ref, i_ref, z_ref)):
        ref[...] = jnp.dot(h, w_ref[:, idx * d:(idx + 1) * d], preferred_element_type=F32)


def _hg_in_call(x, gain, w, tm):
    n = x.shape[0]
    row = pl.BlockSpec((tm, D_MODEL), lambda i: (i, 0))
    out = jax.ShapeDtypeStruct((n, D_MODEL), F32)
    return pl.pallas_call(
        _hg_in_kernel,
        grid=(n // tm,),
        in_specs=[row, _const_spec((1, D_MODEL)), _const_spec((D_MODEL, 4 * D_MODEL))],
        out_specs=[row, row, row, row],
        out_shape=[out, out, out, out],
        compiler_params=_params("arbitrary"),
        name="hgrn_in_proj",
    )(x, gain, w)


def _hg_scan_kernel(*refs, chunk, sub, has_s0):
    st_refs = refs[-HG_HEADS:]
    refs = refs[:-HG_HEADS]
    if has_s0:
        q_ref, f_ref, i_ref, z_ref, lb_ref, on_ref, s0_ref, y_ref, sf_ref = refs
    else:
        q_ref, f_ref, i_ref, z_ref, lb_ref, on_ref, y_ref, sf_ref = refs
    t = pl.program_id(1)

    @pl.when(t == 0)
    def _():
        for hd in range(HG_HEADS):
            st_refs[hd][...] = s0_ref[hd] if has_s0 else jnp.zeros((HG_EXPAND, HG_EXPAND), F32)

    lb = lb_ref[...]
    forget = lb + (1.0 - lb) * _sigmoid(f_ref[...])
    log_f = jnp.log(forget)
    key_all = 1.0 - forget
    r = lax.broadcasted_iota(I32, (chunk, chunk), 0)
    c = lax.broadcasted_iota(I32, (chunk, chunk), 1)
    causal = c <= r
    tri = causal.astype(BF16)
    hi = log_f.astype(BF16)
    rest = log_f - hi.astype(F32)
    mid = rest.astype(BF16)
    lo = (rest - mid.astype(F32)).astype(BF16)
    g_all = jnp.dot(jnp.concatenate([tri, tri, tri], axis=1), jnp.concatenate([hi, mid, lo], axis=0),
                    preferred_element_type=F32)

    n_sub = chunk // sub
    row = lax.broadcasted_iota(I32, (chunk, HG_EXPAND), 0)
    for hd in range(HG_HEADS):
        sl = slice(hd * HG_EXPAND, (hd + 1) * HG_EXPAND)
        g = g_all[:, sl]
        q = q_ref[:, sl] * (HG_EXPAND ** -0.5)
        k = key_all[:, sl]
        v = i_ref[:, sl].astype(BF16)
        g_last = g[chunk - 1:chunk, :]
        state = st_refs[hd][...]
        q_in = (q * jnp.exp(g)).astype(BF16)
        k_end = (k * jnp.exp(g_last - g)).astype(BF16)
        o_inter = lax.dot_general(q_in, state.astype(BF16), _NT, preferred_element_type=F32)
        st_refs[hd][...] = state * jnp.exp(g_last) + lax.dot_general(v, k_end, _TN, preferred_element_type=F32)

        g_mids = [g[b * sub + sub // 2 - 1:b * sub + sub // 2, :] for b in range(n_sub)]
        g_mid_rows = jnp.concatenate([jnp.broadcast_to(m, (sub, HG_EXPAND)) for m in g_mids], axis=0)
        q_sub = (q * jnp.exp(g - g_mid_rows)).astype(BF16)
        k_sub = jnp.concatenate(
            [jnp.where(row < (b + 1) * sub, k * jnp.exp(g_mids[b] - g), 0.0) for b in range(n_sub)], axis=0)
        a_all = lax.dot_general(q_sub, k_sub.astype(BF16), _NT, preferred_element_type=F32)
        a = a_all[:, 0:chunk]
        for b in range(1, n_sub):
            a = jnp.where(r >= b * sub, a_all[:, b * chunk:(b + 1) * chunk], a)
        a = jnp.where(causal, a, 0.0).astype(BF16)
        o = o_inter + jnp.dot(a, v, preferred_element_type=F32)
        o = o * lax.rsqrt(jnp.mean(o * o, axis=-1, keepdims=True) + RMS_EPS) * on_ref[:, sl]
        z = z_ref[:, sl]
        y_ref[:, sl] = (o * (z * _sigmoid(z))).astype(BF16)

    @pl.when(t == pl.num_programs(1) - 1)
    def _():
        for hd in range(HG_HEADS):
            sf_ref[hd] = st_refs[hd][...]


def _hg_scan_call(q, f, i, z, lb, out_norm, s0_t, batch, seq, chunk, sub):
    n_chunks = seq // chunk
    has_s0 = s0_t is not None
    row = pl.BlockSpec((chunk, D_MODEL), lambda b, t: (b * n_chunks + t, 0))
    st = pl.BlockSpec((None, HG_HEADS, HG_EXPAND, HG_EXPAND), lambda b, t: (b, 0, 0, 0))
    in_specs = [row, row, row, row, _const_spec((1, D_MODEL)), _const_spec((1, D_MODEL))]
    args = [q, f, i, z, lb, out_norm]
    if has_s0:
        in_specs.append(st)
        args.append(s0_t)
    return pl.pallas_call(
        functools.partial(_hg_scan_kernel, chunk=chunk, sub=sub, has_s0=has_s0),
        grid=(batch, n_chunks),
        in_specs=in_specs,
        out_specs=[row, st],
        out_shape=[jax.ShapeDtypeStruct((batch * seq, D_MODEL), BF16),
                   jax.ShapeDtypeStruct((batch, HG_HEADS, HG_EXPAND, HG_EXPAND), F32)],
        scratch_shapes=[pltpu.VMEM((HG_EXPAND, HG_EXPAND), F32)] * HG_HEADS,
        compiler_params=_params("arbitrary", "arbitrary"),
        name="hgrn_scan",
    )(*args)


def _router_kernel(y_ref, wo_ref, x_ref, g_ref, wr_ref, xo_ref, h_ref, meta_ref, cnt_ref):
    tr = x_ref.shape[0]
    x = x_ref[...] + jnp.dot(y_ref[...], wo_ref[...], preferred_element_type=F32)
    xo_ref[...] = x
    hf = _rms(x, g_ref[...])
    hi = hf.astype(BF16)
    lo = (hf - hi.astype(F32)).astype(BF16)
    h_ref[...] = hi
    logits = jnp.dot(jnp.concatenate([hi, lo, hi], axis=1), wr_ref[...], preferred_element_type=F32)
    lane = lax.broadcasted_iota(I32, (tr, LANES), 1).astype(F32)
    lg = jnp.where(lane < N_EXPERTS, logits, -jnp.inf)
    m1 = jnp.max(lg, axis=-1, keepdims=True)
    i1 = jnp.min(jnp.where(lg == m1, lane, float(LANES)), axis=-1, keepdims=True)
    lg2 = jnp.where(lane == i1, -jnp.inf, lg)
    m2 = jnp.max(lg2, axis=-1, keepdims=True)
    i2 = jnp.min(jnp.where(lg2 == m2, lane, float(LANES)), axis=-1, keepdims=True)
    e = jnp.exp(m2 - m1)
    gate1 = 1.0 / (1.0 + e)
    gate2 = e / (1.0 + e)

    chosen = jnp.logical_or(lane == i1, lane == i2)
    r = lax.broadcasted_iota(I32, (tr, tr), 0)
    c = lax.broadcasted_iota(I32, (tr, tr), 1)
    before = (c < r).astype(BF16)
    rank = jnp.dot(before, chosen.astype(BF16), preferred_element_type=F32)
    counts = jnp.sum(chosen.astype(F32), axis=0, keepdims=True)
    padded = jnp.floor((counts + (ROUTE_ALIGN - 1)) * (1.0 / ROUTE_ALIGN)) * ROUTE_ALIGN
    r2 = lax.broadcasted_iota(I32, (LANES, LANES), 0)
    c2 = lax.broadcasted_iota(I32, (LANES, LANES), 1)
    run_start = jnp.dot(jnp.broadcast_to(padded, (8, LANES)).astype(BF16), (r2 < c2).astype(BF16),
                        preferred_element_type=F32)[0:1]
    slot = run_start + rank
    slot1 = jnp.sum(jnp.where(lane == i1, slot, 0.0), axis=-1, keepdims=True)
    slot2 = jnp.sum(jnp.where(lane == i2, slot, 0.0), axis=-1, keepdims=True)
    meta = jnp.where(lane == 0, slot1, 0.0)
    for idx, val in ((1, slot2), (2, gate1), (3, gate2), (4, i1), (5, i2)):
        meta = jnp.where(lane == idx, val, meta)
    meta_ref[...] = meta
    cnt_ref[...] = jnp.broadcast_to(counts.astype(I32)[None], cnt_ref.shape)


def _router_call(y, w_out, x, gain, w_router_stack):
    n = x.shape[0]
    n_tiles = n // ROUTE_TILE
    row = pl.BlockSpec((ROUTE_TILE, D_MODEL), lambda i: (i, 0))
    return pl.pallas_call(
        _router_kernel,
        grid=(n_tiles,),
        in_specs=[row, _const_spec((D_MODEL, D_MODEL)), row, _const_spec((1, D_MODEL)),
                  _const_spec((3 * D_MODEL, LANES))],
        out_specs=[row, row, pl.BlockSpec((ROUTE_TILE, LANES), lambda i: (i, 0)),
                   pl.BlockSpec((1, 8, LANES), lambda i: (i, 0, 0))],
        out_shape=[jax.ShapeDtypeStruct((n, D_MODEL), F32),
                   jax.ShapeDtypeStruct((n, D_MODEL), BF16),
                   jax.ShapeDtypeStruct((n, LANES), F32),
                   jax.ShapeDtypeStruct((n_tiles, 8, LANES), I32)],
        compiler_params=_params("arbitrary"),
        name="moe_router",
    )(y, w_out, x, gain, w_router_stack)


def _run_copies(tbl_ref, tile, stage_ref, hbm_ref, sem, to_hbm):
    base = tile * (3 * N_EXPERTS)
    for e in range(N_EXPERTS):
        lo = tbl_ref[base + e]
        n = tbl_ref[base + N_EXPERTS + e]
        dst = tbl_ref[base + 2 * N_EXPERTS + e]
        for sz in RUN_SIZES:
            done = jnp.bitwise_and(n, ~(2 * sz - 1))
            st = stage_ref.at[pl.ds(pl.multiple_of(lo + done, ROUTE_ALIGN), sz)]
            hb = hbm_ref.at[pl.ds(pl.multiple_of(dst + done, ROUTE_ALIGN), sz)]
            copy = pltpu.make_async_copy(st, hb, sem) if to_hbm else pltpu.make_async_copy(hb, st, sem)
            yield jnp.bitwise_and(n, sz) != 0, copy


def _start_runs(tbl_ref, tile, stage_ref, hbm_ref, sem, to_hbm):
    for cond, copy in _run_copies(tbl_ref, tile, stage_ref, hbm_ref, sem, to_hbm):
        pl.when(cond)(copy.start)


def _wait_runs(tbl_ref, tile, stage_ref, hbm_ref, sem, to_hbm):
    for cond, copy in _run_copies(tbl_ref, tile, stage_ref, hbm_ref, sem, to_hbm):
        pl.when(cond)(copy.wait)


def _dispatch_kernel(tbl_ref, h_ref, meta_ref, xs_ref, stage_ref, sem):
    tile = pl.program_id(0)
    buf = tile % 2
    meta = meta_ref[...]
    slot1 = meta[:, 0:1].astype(I32)
    slot2 = meta[:, 1:2].astype(I32)
    rows = lax.broadcasted_iota(I32, (ROUTE_TILE, ROUTE_ROWS), 1)
    pick = jnp.logical_or(rows == slot1, rows == slot2).astype(BF16)
    stage_ref[buf] = lax.dot_general(pick, h_ref[...], _TN, preferred_element_type=F32).astype(BF16)
    _start_runs(tbl_ref, tile, stage_ref.at[buf], xs_ref, sem.at[buf], True)

    @pl.when(tile > 0)
    def _():
        _wait_runs(tbl_ref, tile - 1, stage_ref.at[1 - buf], xs_ref, sem.at[1 - buf], True)

    @pl.when(tile == pl.num_programs(0) - 1)
    def _():
        _wait_runs(tbl_ref, tile, stage_ref.at[buf], xs_ref, sem.at[buf], True)


def _dispatch_call(tbl, h, meta, sorted_rows):
    n = h.shape[0]
    grid_spec = pltpu.PrefetchScalarGridSpec(
        num_scalar_prefetch=1,
        grid=(n // ROUTE_TILE,),
        in_specs=[pl.BlockSpec((ROUTE_TILE, D_MODEL), lambda i, tbl: (i, 0)),
                  pl.BlockSpec((ROUTE_TILE, LANES), lambda i, tbl: (i, 0))],
        out_specs=pl.BlockSpec(memory_space=pl.ANY),
        scratch_shapes=[pltpu.VMEM((2, ROUTE_ROWS, D_MODEL), BF16), pltpu.SemaphoreType.DMA((2,))],
    )
    return pl.pallas_call(
        _dispatch_kernel,
        grid_spec=grid_spec,
        out_shape=jax.ShapeDtypeStruct((sorted_rows, D_MODEL), BF16),
        compiler_params=_params("arbitrary"),
        name="moe_dispatch",
    )(tbl, h, meta)


def _expert_kernel(te_ref, trow_ref, nval_ref, x_ref, wg_ref, wu_ref, wd_ref, o_ref, act_ref, acc_ref):
    i = pl.program_id(0)
    j = pl.program_id(1)
    nv = nval_ref[i]

    @pl.when(nv > 0)
    def _():
        @pl.when(j == 0)
        def _():
            acc_ref[...] = jnp.zeros_like(acc_ref)

        x = x_ref[...]
        row = lax.broadcasted_iota(I32, x.shape, 0)
        x = jnp.where(row < nv, x, jnp.zeros_like(x))
        acc_ref[...] += _swiglu_block(x, wg_ref, wu_ref, wd_ref, act_ref)

        @pl.when(j == pl.num_programs(1) - 1)
        def _():
            o_ref[...] = acc_ref[...].astype(BF16)


def _expert_call(te, trow, nval, xs, w_gu, w_down, fc):
    n_tiles = te.shape[0]
    d_ff = w_down.shape[1]
    nf = d_ff // fc

    def jj(i, j, nval):
        return jnp.where(nval[i] > 0, j, nf - 1)

    grid_spec = pltpu.PrefetchScalarGridSpec(
        num_scalar_prefetch=3,
        grid=(n_tiles, nf),
        in_specs=[pl.BlockSpec((EXPERT_TILE, D_MODEL), lambda i, j, te, trow, nval: (trow[i], 0)),
                  pl.BlockSpec((None, D_MODEL, fc), lambda i, j, te, trow, nval: (te[i], 0, jj(i, j, nval))),
                  pl.BlockSpec((None, D_MODEL, fc),
                               lambda i, j, te, trow, nval: (te[i], 0, jj(i, j, nval) + nf)),
                  pl.BlockSpec((None, fc, D_MODEL), lambda i, j, te, trow, nval: (te[i], jj(i, j, nval), 0))],
        out_specs=pl.BlockSpec((EXPERT_TILE, D_MODEL), lambda i, j, te, trow, nval: (trow[i], 0)),
        scratch_shapes=[pltpu.VMEM((EXPERT_TILE, fc), BF16), pltpu.VMEM((EXPERT_TILE, D_MODEL), F32)],
    )
    return pl.pallas_call(
        _expert_kernel,
        grid_spec=grid_spec,
        out_shape=jax.ShapeDtypeStruct(xs.shape, BF16),
        compiler_params=_params("arbitrary", "arbitrary"),
        name="moe_experts",
    )(te, trow, nval, xs, w_gu, w_gu, w_down)


def _combine_kernel(tbl_ref, ys_ref, meta_ref, x_ref, g_ref, o_ref, stage_ref, sem):
    tile = pl.program_id(0)
    buf = tile % 2

    @pl.when(tile == 0)
    def _():
        stage_ref[...] = jnp.zeros_like(stage_ref)
        _start_runs(tbl_ref, tile, stage_ref.at[buf], ys_ref, sem.at[buf], False)

    @pl.when(tile + 1 < pl.num_programs(0))
    def _():
        _start_runs(tbl_ref, tile + 1, stage_ref.at[1 - buf], ys_ref, sem.at[1 - buf], False)

    _wait_runs(tbl_ref, tile, stage_ref.at[buf], ys_ref, sem.at[buf], False)
    meta = meta_ref[...]
    rows = lax.broadcasted_iota(I32, (ROUTE_TILE, ROUTE_ROWS), 1)
    stage = stage_ref[buf]
    y1 = jnp.dot((rows == meta[:, 0:1].astype(I32)).astype(BF16), stage, preferred_element_type=F32)
    y2 = jnp.dot((rows == meta[:, 1:2].astype(I32)).astype(BF16), stage, preferred_element_type=F32)
    x = x_ref[...] + (meta[:, 2:3] * y1 + meta[:, 3:4] * y2)
    o_ref[...] = _rms(x, g_ref[...])


def _combine_call(tbl, ys, meta, x, gain):
    n = x.shape[0]
    row = pl.BlockSpec((ROUTE_TILE, D_MODEL), lambda i, tbl: (i, 0))
    grid_spec = pltpu.PrefetchScalarGridSpec(
        num_scalar_prefetch=1,
        grid=(n // ROUTE_TILE,),
        in_specs=[pl.BlockSpec(memory_space=pl.ANY),
                  pl.BlockSpec((ROUTE_TILE, LANES), lambda i, tbl: (i, 0)),
                  row,
                  pl.BlockSpec((1, D_MODEL), lambda i, tbl: (0, 0))],
        out_specs=row,
        scratch_shapes=[pltpu.VMEM((2, ROUTE_ROWS, D_MODEL), BF16), pltpu.SemaphoreType.DMA((2,))],
    )
    return pl.pallas_call(
        _combine_kernel,
        grid_spec=grid_spec,
        out_shape=jax.ShapeDtypeStruct((n, D_MODEL), F32),
        compiler_params=_params("arbitrary"),
        name="moe_combine",
    )(tbl, ys, meta, x, gain)


def _routing_tables(counts, n_row_tiles):
    padded = (counts + (ROUTE_ALIGN - 1)) // ROUTE_ALIGN * ROUTE_ALIGN
    stage_off = jnp.cumsum(padded, axis=1) - padded
    total = jnp.sum(padded, axis=0)
    tiles_e = (total + (EXPERT_TILE - 1)) // EXPERT_TILE
    region = jnp.cumsum(tiles_e) - tiles_e
    dst = region[None, :] * EXPERT_TILE + jnp.cumsum(padded, axis=0) - padded
    tbl = jnp.concatenate([stage_off, padded, dst], axis=1).reshape(-1).astype(I32)

    used = jnp.sum(tiles_e)
    ids = jnp.arange(n_row_tiles, dtype=I32)
    ends = jnp.cumsum(tiles_e)
    te = jnp.sum((ids[:, None] >= ends[None, :]).astype(I32), axis=1)
    te = jnp.minimum(te, N_EXPERTS - 1)
    nval = jnp.clip(total[te] - (ids - region[te]) * EXPERT_TILE, 0, EXPERT_TILE)
    valid = ids < used
    nval = jnp.where(valid, nval, 0).astype(I32)
    last = jnp.maximum(used - 1, 0)
    te = jnp.where(valid, te, te[last]).astype(I32)
    trow = jnp.where(valid, ids, n_row_tiles).astype(I32)
    return tbl, te, trow, nval


def _moe_final(y, w_out, x, gain, w_router_stack, w_gu, w_down, final_gain, fc):
    n = x.shape[0]
    n_tiles = n // ROUTE_TILE
    max_rows = 2 * n + n_tiles * N_EXPERTS * (ROUTE_ALIGN - 1)
    n_row_tiles = -(-max_rows // EXPERT_TILE) + N_EXPERTS
    x, h, meta, cnt = _router_call(y, w_out, x, gain, w_router_stack)
    tbl, te, trow, nval = _routing_tables(cnt[:, 0, :N_EXPERTS], n_row_tiles)
    xs = _dispatch_call(tbl, h, meta, (n_row_tiles + 1) * EXPERT_TILE)
    ys = _expert_call(te, trow, nval, xs, w_gu, w_down, fc)
    return _combine_call(tbl, ys, meta, x, final_gain)


def _stream(x, cache_k, cache_v, s0_t, lb1, weights, tm, chunk, sub):
    (norm_mix, norm_ch, norm_final, w_qkv, w_o, rel_table, w_in, w_out, out_norm,
     w_gu, w_down, w_router, w_egu, w_edown) = weights
    batch, seq, _ = x.shape
    n = batch * seq
    x = x.reshape(n, D_MODEL)
    row = lambda a: a.reshape(1, D_MODEL)

    if cache_k is None:
        q, k, v, kf, vf = _qkv_call(x, row(norm_mix[0]), w_qkv, seq // tm, tm)
        att = _band_attn_call(q, k, v, _band_bias(rel_table), batch, seq)
        keep = min(WINDOW, seq)
        new_k = kf.reshape(batch, keep, ATT_HEADS, ATT_HEAD_DIM)
        new_v = vf.reshape(batch, keep, ATT_HEADS, ATT_HEAD_DIM)
    else:
        q, k, v, kf, vf = _qkv_call(x, row(norm_mix[0]), w_qkv, 1, tm)
        w = cache_k.shape[1]
        att = _cache_attn_call(q, k, v, cache_k.reshape(batch, w, D_MODEL), cache_v.reshape(batch, w, D_MODEL),
                               _cache_bias(rel_table, w, seq), batch, seq)
        new_k = kf.reshape(batch, seq, ATT_HEADS, ATT_HEAD_DIM)
        new_v = vf.reshape(batch, seq, ATT_HEADS, ATT_HEAD_DIM)
    x = _proj_residual_call(att, w_o, x, tm)
    x = _ffn_call(x, row(norm_ch[0]), w_gu, w_down, min(n, 2 * tm), w_down.shape[0] // 2)

    q, f, i, z = _hg_in_call(x, row(norm_mix[1]), w_in, min(tm, 256))
    y, s_t = _hg_scan_call(q, f, i, z, lb1, row(out_norm), s0_t, batch, seq, chunk, sub)
    y = _moe_final(y, w_out, x, row(norm_ch[1]), w_router, w_egu, w_edown, row(norm_final),
                   w_edown.shape[1] // 2)
    return y.reshape(batch, seq, D_MODEL), new_k, new_v, jnp.swapaxes(s_t, -1, -2)


def kernel(x_prompt, x_sample, cache_k, cache_v, state_hgrn, norm_mix, norm_ch, norm_final, att_w_qkv, att_w_o, att_rel_bias, hg_w_in, hg_w_out, hg_out_norm, hg_lower_bounds, ffn_w_gu, ffn_w_down, moe_w_router, moe_w_gu, moe_w_down):
    lb = jnp.cumsum(jax.nn.softmax(hg_lower_bounds.astype(F32), axis=0), axis=0)
    lb1 = (lb[1] - lb[0]).reshape(1, D_MODEL)
    w_router = jnp.pad(moe_w_router[0].astype(F32), ((0, 0), (0, LANES - N_EXPERTS)))
    w_router_hi = w_router.astype(BF16)
    w_router_lo = (w_router - w_router_hi.astype(F32)).astype(BF16)
    w_router = jnp.concatenate([w_router_hi, w_router_hi, w_router_lo], axis=0)
    weights = (norm_mix.astype(F32), norm_ch.astype(F32), norm_final.astype(F32),
               att_w_qkv[0].astype(BF16), att_w_o[0].astype(BF16), att_rel_bias[0],
               hg_w_in[0].astype(BF16), hg_w_out[0].astype(BF16), hg_out_norm[0].astype(F32),
               ffn_w_gu[0].astype(BF16), ffn_w_down[0].astype(BF16),
               w_router, moe_w_gu[0].astype(BF16), moe_w_down[0].astype(BF16))

    yp, kp, vp, sp = _stream(x_prompt, None, None, None, lb1, weights, 512, 128, 32)
    s0_t = jnp.swapaxes(state_hgrn[0].astype(F32), -1, -2)
    dec_tokens = x_sample.shape[0] * x_sample.shape[1]
    ys, ks, vs, ss = _stream(x_sample, cache_k[0], cache_v[0], s0_t, lb1, weights,
                             dec_tokens, x_sample.shape[1], x_sample.shape[1])
    return (yp, ys, kp[None], vp[None], ks[None], vs[None], sp[None], ss[None])
```

```python
import functools

import jax
import jax.numpy as jnp
import numpy as np
from jax import lax
from jax.experimental import pallas as pl
from jax.experimental.pallas import tpu as pltpu

F32 = jnp.float32
BF16 = jnp.bfloat16
I32 = jnp.int32

D_MODEL = 1024
ATT_HEADS = 16
ATT_HEAD_DIM = 64
CHUNK = 64
LEFT_CHUNKS = 8
WINDOW = LEFT_CHUNKS * CHUNK
REL_CLIP = 128
HG_HEADS = 8
HG_EXPAND = 128
N_EXPERTS = 8
RMS_EPS = 1e-6

LANES = 128
BF16_SUBLANES = 16
MXU_COLS = 256
VMEM_LIMIT = 56 * 1024 * 1024

HEAD_PAIRS = ATT_HEADS // 2
Q_GROUP = 2 * WINDOW
Q_HALF = WINDOW // 2
K_SPAN = Q_HALF + WINDOW
PAIRS_PER_STEP = 2
MASKED = -1e30

ROUTE_TILE = 512
ROUTE_ALIGN = BF16_SUBLANES
ROUTE_ROWS = 2 * ROUTE_TILE + N_EXPERTS * (ROUTE_ALIGN - 1)
ROUTE_ROWS = -(-ROUTE_ROWS // LANES) * LANES
RUN_SIZES = tuple(ROUTE_TILE >> s for s in range(6))
EXPERT_TILE = 1024

SCAN_CHUNKS_PER_STEP = 4

_NT = (((1,), (1,)), ((), ()))
_TN = (((0,), (0,)), ((), ()))


def _params(*sem):
    return pltpu.CompilerParams(dimension_semantics=sem, vmem_limit_bytes=VMEM_LIMIT)


def _rms(x, gain):
    return x * lax.rsqrt(jnp.mean(x * x, axis=-1, keepdims=True) + RMS_EPS) * gain


def _sigmoid(x):
    return 1.0 / (1.0 + jnp.exp(-x))


def _const_spec(shape):
    zeros = (0,) * len(shape)
    return pl.BlockSpec(shape, lambda *_: zeros)


def _qkv_kernel(x_ref, g_ref, w_ref, q_ref, k_ref, v_ref, kf_ref, vf_ref, *, tiles_per_seq):
    h = _rms(x_ref[...], g_ref[...]).astype(BF16)
    d = D_MODEL
    q = jnp.dot(h, w_ref[:, 0:d], preferred_element_type=F32) * (ATT_HEAD_DIM ** -0.5)
    k = jnp.dot(h, w_ref[:, d:2 * d], preferred_element_type=F32)
    v = jnp.dot(h, w_ref[:, 2 * d:3 * d], preferred_element_type=F32)
    for p in range(HEAD_PAIRS):
        sl = slice(p * LANES, (p + 1) * LANES)
        q_ref[p] = q[:, sl].astype(BF16)
        k_ref[p] = k[:, sl].astype(BF16)
        v_ref[p] = v[:, sl].astype(BF16)

    @pl.when(pl.program_id(0) % tiles_per_seq == tiles_per_seq - 1)
    def _():
        kf_ref[...] = k
        vf_ref[...] = v


def _qkv_call(x, gain, w, tiles_per_seq, tm):
    n = x.shape[0]
    n_tiles = n // tm
    n_seq = n_tiles // tiles_per_seq
    slab = jax.ShapeDtypeStruct((HEAD_PAIRS, n, LANES), BF16)
    tail = jax.ShapeDtypeStruct((n_seq * tm, D_MODEL), F32)
    slab_spec = pl.BlockSpec((HEAD_PAIRS, tm, LANES), lambda i: (0, i, 0))
    tail_spec = pl.BlockSpec((tm, D_MODEL), lambda i: (i // tiles_per_seq, 0))
    return pl.pallas_call(
        functools.partial(_qkv_kernel, tiles_per_seq=tiles_per_seq),
        grid=(n_tiles,),
        in_specs=[pl.BlockSpec((tm, D_MODEL), lambda i: (i, 0)),
                  _const_spec((1, D_MODEL)),
                  _const_spec((D_MODEL, 3 * D_MODEL))],
        out_specs=[slab_spec, slab_spec, slab_spec, tail_spec, tail_spec],
        out_shape=[slab, slab, slab, tail, tail],
        compiler_params=_params("arbitrary"),
        name="qkv_proj",
    )(x, gain, w)


def _band_attn_kernel(q_ref, kp_ref, kc_ref, vp_ref, vc_ref, bias_ref, o_ref):
    first_group = pl.program_id(2) == 0
    lane = lax.broadcasted_iota(I32, (2 * Q_HALF, LANES), 1)
    row = lax.broadcasted_iota(I32, (2 * Q_HALF, LANES), 0)
    own = (lane < ATT_HEAD_DIM) == (row < Q_HALF)
    first_head = lax.broadcasted_iota(I32, (Q_HALF, LANES), 1) < ATT_HEAD_DIM
    def window(prev_ref, cur_ref, pr, lo):
        hi = lo + K_SPAN
        if lo >= WINDOW:
            return cur_ref[pr, lo - WINDOW:hi - WINDOW, :]
        return jnp.concatenate([prev_ref[pr, lo:, :], cur_ref[pr, 0:hi - WINDOW, :]], axis=0)

    for pr in range(PAIRS_PER_STEP):
        for part in range(q_ref.shape[1] // Q_HALF):
            rows = slice(part * Q_HALF, (part + 1) * Q_HALF)
            qh = q_ref[pr, rows, :]
            q2 = jnp.concatenate([qh, qh], axis=0)
            q2 = jnp.where(own, q2, jnp.zeros_like(q2))
            keys = window(kp_ref, kc_ref, pr, part * Q_HALF)
            vals = window(vp_ref, vc_ref, pr, part * Q_HALF)
            s = lax.dot_general(q2, keys, _NT, preferred_element_type=F32)
            if (part + 1) * Q_HALF <= WINDOW:
                s = s + bias_ref[pr, jnp.where(first_group, 1 + part, 0)]
            else:
                s = s + bias_ref[pr, 0]
            m = jnp.max(s, axis=-1, keepdims=True)
            p = jnp.exp(s - m)
            l = jnp.sum(p, axis=-1, keepdims=True)
            o2 = jnp.dot(p.astype(BF16), vals, preferred_element_type=F32) / l
            o = jnp.where(first_head, o2[:Q_HALF], o2[Q_HALF:])
            o_ref[rows, pr * LANES:(pr + 1) * LANES] = o.astype(BF16)


def _band_bias(table):
    qi = np.arange(Q_HALF)[:, None]
    kj = np.arange(K_SPAN)[None, :]
    cq, ck = qi // CHUNK, kj // CHUNK
    band = (ck >= cq) & (ck <= cq + LEFT_CHUNKS)
    valid = np.stack([band, band & (kj >= WINDOW), band & (kj >= WINDOW - Q_HALF)])
    table = table.astype(F32)
    n_rel = table.shape[1]
    span = Q_HALF + K_SPAN - 1
    lo_rep = (K_SPAN - 1 - WINDOW) - (CHUNK - 1)
    hi_rep = span - lo_rep - n_rel
    line = jnp.concatenate([jnp.repeat(table[:, :1], lo_rep, axis=1), table,
                            jnp.repeat(table[:, -1:], hi_rep, axis=1)], axis=1)[:, ::-1]
    flow = jnp.tile(line, (1, Q_HALF + 1))[:, Q_HALF - 1:Q_HALF - 1 + Q_HALF * (span - 1)]
    b = flow.reshape(-1, Q_HALF, span - 1)[:, :, :K_SPAN]
    b = jnp.where(valid[None], b[:, None], MASKED).reshape(HEAD_PAIRS, 2, 3, Q_HALF, K_SPAN)
    return jnp.swapaxes(b, 1, 2).reshape(HEAD_PAIRS, 3, 2 * Q_HALF, K_SPAN)


def _band_attn_call(q, k, v, bias, batch, seq):
    groups = seq // Q_GROUP
    n = batch * seq

    def cur(p, b, g):
        return (p, b * groups + g, 0)

    def prev(p, b, g):
        return (p, (b * groups + g) * (Q_GROUP // WINDOW) - jnp.minimum(g, 1), 0)

    blk = (PAIRS_PER_STEP, Q_GROUP, LANES)
    blk_prev = (PAIRS_PER_STEP, WINDOW, LANES)
    return pl.pallas_call(
        _band_attn_kernel,
        grid=(HEAD_PAIRS // PAIRS_PER_STEP, batch, groups),
        in_specs=[pl.BlockSpec(blk, cur), pl.BlockSpec(blk_prev, prev), pl.BlockSpec(blk, cur),
                  pl.BlockSpec(blk_prev, prev), pl.BlockSpec(blk, cur),
                  pl.BlockSpec((PAIRS_PER_STEP, 3, 2 * Q_HALF, K_SPAN), lambda p, b, g: (p, 0, 0, 0))],
        out_specs=pl.BlockSpec((Q_GROUP, PAIRS_PER_STEP * LANES), lambda p, b, g: (b * groups + g, p)),
        out_shape=jax.ShapeDtypeStruct((n, D_MODEL), BF16),
        compiler_params=_params("arbitrary", "arbitrary", "arbitrary"),
        name="band_attention",
    )(q, k, k, v, v, bias)


def _cache_attn_kernel(q_ref, k_ref, v_ref, ck_ref, cv_ref, bias_ref, o_ref):
    s_len = q_ref.shape[1]
    lane = lax.broadcasted_iota(I32, (s_len, LANES), 1)
    low = lane < ATT_HEAD_DIM
    for p in range(HEAD_PAIRS):
        sl = slice(p * LANES, (p + 1) * LANES)
        keys = jnp.concatenate([ck_ref[:, sl].astype(BF16), k_ref[p]], axis=0)
        vals = jnp.concatenate([cv_ref[:, sl].astype(BF16), v_ref[p]], axis=0)
        qp = q_ref[p]
        outs = []
        for hd in range(2):
            qm = jnp.where(low if hd == 0 else jnp.logical_not(low), qp, jnp.zeros_like(qp))
            s = lax.dot_general(qm, keys, _NT, preferred_element_type=F32) + bias_ref[2 * p + hd]
            m = jnp.max(s, axis=-1, keepdims=True)
            e = jnp.exp(s - m)
            l = jnp.sum(e, axis=-1, keepdims=True)
            outs.append(jnp.dot(e.astype(BF16), vals, preferred_element_type=F32) / l)
        o_ref[:, sl] = jnp.where(low, outs[0], outs[1]).astype(BF16)


def _cache_bias(table, w, s_len):
    qi = np.arange(s_len)[:, None]
    kj = np.arange(w + s_len)[None, :]
    idx = np.clip(w + qi - kj, -(CHUNK - 1), REL_CLIP) + (CHUNK - 1)
    return table.astype(F32)[:, idx]


def _cache_attn_call(q, k, v, cache_k, cache_v, bias, batch, s_len):
    w = cache_k.shape[1]
    slab = pl.BlockSpec((HEAD_PAIRS, s_len, LANES), lambda b: (0, b, 0))
    cache = pl.BlockSpec((None, w, D_MODEL), lambda b: (b, 0, 0))
    return pl.pallas_call(
        _cache_attn_kernel,
        grid=(batch,),
        in_specs=[slab, slab, slab, cache, cache, _const_spec((ATT_HEADS, s_len, w + s_len))],
        out_specs=pl.BlockSpec((s_len, D_MODEL), lambda b: (b, 0)),
        out_shape=jax.ShapeDtypeStruct((batch * s_len, D_MODEL), BF16),
        compiler_params=_params("arbitrary"),
        name="cache_attention",
    )(q, k, v, cache_k, cache_v, bias)


def _proj_residual_kernel(a_ref, w_ref, x_ref, o_ref):
    o_ref[...] = x_ref[...] + jnp.dot(a_ref[...], w_ref[...], preferred_element_type=F32)


def _proj_residual_call(a, w, x, tm):
    n = x.shape[0]
    row = pl.BlockSpec((tm, D_MODEL), lambda i: (i, 0))
    return pl.pallas_call(
        _proj_residual_kernel,
        grid=(n // tm,),
        in_specs=[row, _const_spec((D_MODEL, D_MODEL)), row],
        out_specs=row,
        out_shape=jax.ShapeDtypeStruct((n, D_MODEL), F32),
        compiler_params=_params("arbitrary"),
        name="proj_residual",
    )(a, w, x)


def _swiglu_block(h, wg_ref, wu_ref, wd_ref, act_ref):
    fc = act_ref.shape[1]
    for c0 in range(0, fc, MXU_COLS):
        sl = slice(c0, min(c0 + MXU_COLS, fc))
        gate = jnp.dot(h, wg_ref[:, sl], preferred_element_type=F32)
        up = jnp.dot(h, wu_ref[:, sl], preferred_element_type=F32)
        act_ref[:, sl] = (gate * _sigmoid(gate) * up).astype(BF16)
    return jnp.dot(act_ref[...], wd_ref[...], preferred_element_type=F32)


def _ffn_kernel(x_ref, g_ref, wg_ref, wu_ref, wd_ref, o_ref, h_ref, act_ref, acc_ref):
    j = pl.program_id(1)

    @pl.when(j == 0)
    def _():
        h_ref[...] = _rms(x_ref[...], g_ref[...]).astype(BF16)
        acc_ref[...] = jnp.zeros_like(acc_ref)

    acc_ref[...] += _swiglu_block(h_ref[...], wg_ref, wu_ref, wd_ref, act_ref)

    @pl.when(j == pl.num_programs(1) - 1)
    def _():
        o_ref[...] = x_ref[...] + acc_ref[...]


def _ffn_call(x, gain, w_gu, w_down, tm, fc):
    n = x.shape[0]
    d_ff = w_down.shape[0]
    nf = d_ff // fc
    row = pl.BlockSpec((tm, D_MODEL), lambda i, j: (i, 0))
    return pl.pallas_call(
        _ffn_kernel,
        grid=(n // tm, nf),
        in_specs=[row, _const_spec((1, D_MODEL)),
                  pl.BlockSpec((D_MODEL, fc), lambda i, j: (0, j)),
                  pl.BlockSpec((D_MODEL, fc), lambda i, j: (0, j + nf)),
                  pl.BlockSpec((fc, D_MODEL), lambda i, j: (j, 0))],
        out_specs=row,
        out_shape=jax.ShapeDtypeStruct((n, D_MODEL), F32),
        scratch_shapes=[pltpu.VMEM((tm, D_MODEL), BF16), pltpu.VMEM((tm, fc), BF16),
                        pltpu.VMEM((tm, D_MODEL), F32)],
        compiler_params=_params("arbitrary", "arbitrary"),
        name="swiglu_ffn",
    )(x, gain, w_gu, w_gu, w_down)


def _hg_in_kernel(x_ref, g_ref, w_ref, q_ref, f_ref, i_ref, z_ref):
    h = _rms(x_ref[...], g_ref[...]).astype(BF16)
    d = D_MODEL
    for idx, ref in enumerate((q_ref, f_ref, i_ref, z_ref)):
        ref[...] = jnp.dot(h, w_ref[:, idx * d:(idx + 1) * d], preferred_element_type=F32)


def _hg_in_call(x, gain, w, tm):
    n = x.shape[0]
    row = pl.BlockSpec((tm, D_MODEL), lambda i: (i, 0))
    out = jax.ShapeDtypeStruct((n, D_MODEL), F32)
    return pl.pallas_call(
        _hg_in_kernel,
        grid=(n // tm,),
        in_specs=[row, _const_spec((1, D_MODEL)), _const_spec((D_MODEL, 4 * D_MODEL))],
        out_specs=[row, row, row, row],
        out_shape=[out, out, out, out],
        compiler_params=_params("arbitrary"),
        name="hgrn_in_proj",
    )(x, gain, w)


def _hg_scan_kernel(*refs, chunk, sub, has_s0):
    st_refs = refs[-HG_HEADS:]
    refs = refs[:-HG_HEADS]
    if has_s0:
        q_ref, f_ref, i_ref, z_ref, lb_ref, on_ref, s0_ref, y_ref, sf_ref = refs
    else:
        q_ref, f_ref, i_ref, z_ref, lb_ref, on_ref, y_ref, sf_ref = refs
    t = pl.program_id(1)

    @pl.when(t == 0)
    def _():
        for hd in range(HG_HEADS):
            st_refs[hd][...] = s0_ref[hd] if has_s0 else jnp.zeros((HG_EXPAND, HG_EXPAND), F32)

    lb = lb_ref[...]
    r = lax.broadcasted_iota(I32, (chunk, chunk), 0)
    c = lax.broadcasted_iota(I32, (chunk, chunk), 1)
    causal = c <= r
    tri = causal.astype(BF16)
    n_sub = chunk // sub
    row = lax.broadcasted_iota(I32, (chunk, HG_EXPAND), 0)
    for c0 in range(0, q_ref.shape[0], chunk):
        _hg_scan_chunk(slice(c0, c0 + chunk), q_ref, f_ref, i_ref, z_ref, on_ref, y_ref, st_refs,
                       lb, r, causal, tri, row, chunk, sub, n_sub)

    @pl.when(t == pl.num_programs(1) - 1)
    def _():
        for hd in range(HG_HEADS):
            sf_ref[hd] = st_refs[hd][...]


def _hg_scan_chunk(rows, q_ref, f_ref, i_ref, z_ref, on_ref, y_ref, st_refs,
                   lb, r, causal, tri, row, chunk, sub, n_sub):
    forget = lb + (1.0 - lb) * _sigmoid(f_ref[rows, :])
    log_f = jnp.log(forget)
    key_all = 1.0 - forget
    hi = log_f.astype(BF16)
    rest = log_f - hi.astype(F32)
    mid = rest.astype(BF16)
    lo = (rest - mid.astype(F32)).astype(BF16)
    g_all = jnp.dot(jnp.concatenate([tri, tri, tri], axis=1), jnp.concatenate([hi, mid, lo], axis=0),
                    preferred_element_type=F32)

    for hd in range(HG_HEADS):
        sl = slice(hd * HG_EXPAND, (hd + 1) * HG_EXPAND)
        g = g_all[:, sl]
        q = q_ref[rows, sl] * (HG_EXPAND ** -0.5)
        k = key_all[:, sl]
        v = i_ref[rows, sl].astype(BF16)
        g_last = g[chunk - 1:chunk, :]
        state = st_refs[hd][...]
        q_in = (q * jnp.exp(g)).astype(BF16)
        k_end = (k * jnp.exp(g_last - g)).astype(BF16)
        o_inter = lax.dot_general(q_in, state.astype(BF16), _NT, preferred_element_type=F32)
        st_refs[hd][...] = state * jnp.exp(g_last) + lax.dot_general(v, k_end, _TN, preferred_element_type=F32)

        g_mids = [g[b * sub + sub // 2 - 1:b * sub + sub // 2, :] for b in range(n_sub)]
        g_mid_rows = jnp.concatenate([jnp.broadcast_to(m, (sub, HG_EXPAND)) for m in g_mids], axis=0)
        q_sub = (q * jnp.exp(g - g_mid_rows)).astype(BF16)
        k_sub = jnp.concatenate(
            [jnp.where(row < (b + 1) * sub, k * jnp.exp(g_mids[b] - g), 0.0) for b in range(n_sub)], axis=0)
        a_all = lax.dot_general(q_sub, k_sub.astype(BF16), _NT, preferred_element_type=F32)
        a = a_all[:, 0:chunk]
        for b in range(1, n_sub):
            a = jnp.where(r >= b * sub, a_all[:, b * chunk:(b + 1) * chunk], a)
        a = jnp.where(causal, a, 0.0).astype(BF16)
        o = o_inter + jnp.dot(a, v, preferred_element_type=F32)
        o = o * lax.rsqrt(jnp.mean(o * o, axis=-1, keepdims=True) + RMS_EPS) * on_ref[:, sl]
        z = z_ref[rows, sl]
        y_ref[rows, sl] = (o * (z * _sigmoid(z))).astype(BF16)


def _hg_scan_call(q, f, i, z, lb, out_norm, s0_t, batch, seq, chunk, sub):
    step_rows = min(seq, SCAN_CHUNKS_PER_STEP * chunk)
    n_chunks = seq // step_rows
    has_s0 = s0_t is not None
    row = pl.BlockSpec((step_rows, D_MODEL), lambda b, t: (b * n_chunks + t, 0))
    st = pl.BlockSpec((None, HG_HEADS, HG_EXPAND, HG_EXPAND), lambda b, t: (b, 0, 0, 0))
    in_specs = [row, row, row, row, _const_spec((1, D_MODEL)), _const_spec((1, D_MODEL))]
    args = [q, f, i, z, lb, out_norm]
    if has_s0:
        in_specs.append(st)
        args.append(s0_t)
    return pl.pallas_call(
        functools.partial(_hg_scan_kernel, chunk=chunk, sub=sub, has_s0=has_s0),
        grid=(batch, n_chunks),
        in_specs=in_specs,
        out_specs=[row, st],
        out_shape=[jax.ShapeDtypeStruct((batch * seq, D_MODEL), BF16),
                   jax.ShapeDtypeStruct((batch, HG_HEADS, HG_EXPAND, HG_EXPAND), F32)],
        scratch_shapes=[pltpu.VMEM((HG_EXPAND, HG_EXPAND), F32)] * HG_HEADS,
        compiler_params=_params("arbitrary", "arbitrary"),
        name="hgrn_scan",
    )(*args)


def _router_kernel(y_ref, wo_ref, x_ref, g_ref, wr_ref, xo_ref, h_ref, meta_ref, cnt_ref):
    tr = x_ref.shape[0]
    x = x_ref[...] + jnp.dot(y_ref[...], wo_ref[...], preferred_element_type=F32)
    xo_ref[...] = x
    hf = _rms(x, g_ref[...])
    hi = hf.astype(BF16)
    lo = (hf - hi.astype(F32)).astype(BF16)
    h_ref[...] = hi
    logits = jnp.dot(jnp.concatenate([hi, lo, hi], axis=1), wr_ref[...], preferred_element_type=F32)
    lane = lax.broadcasted_iota(I32, (tr, LANES), 1).astype(F32)
    lg = jnp.where(lane < N_EXPERTS, logits, -jnp.inf)
    m1 = jnp.max(lg, axis=-1, keepdims=True)
    i1 = jnp.min(jnp.where(lg == m1, lane, float(LANES)), axis=-1, keepdims=True)
    lg2 = jnp.where(lane == i1, -jnp.inf, lg)
    m2 = jnp.max(lg2, axis=-1, keepdims=True)
    i2 = jnp.min(jnp.where(lg2 == m2, lane, float(LANES)), axis=-1, keepdims=True)
    e = jnp.exp(m2 - m1)
    gate1 = 1.0 / (1.0 + e)
    gate2 = e / (1.0 + e)

    chosen = jnp.logical_or(lane == i1, lane == i2)
    r = lax.broadcasted_iota(I32, (tr, tr), 0)
    c = lax.broadcasted_iota(I32, (tr, tr), 1)
    before = (c < r).astype(BF16)
    rank = jnp.dot(before, chosen.astype(BF16), preferred_element_type=F32)
    counts = jnp.sum(chosen.astype(F32), axis=0, keepdims=True)
    padded = jnp.floor((counts + (ROUTE_ALIGN - 1)) * (1.0 / ROUTE_ALIGN)) * ROUTE_ALIGN
    r2 = lax.broadcasted_iota(I32, (LANES, LANES), 0)
    c2 = lax.broadcasted_iota(I32, (LANES, LANES), 1)
    run_start = jnp.dot(jnp.broadcast_to(padded, (8, LANES)).astype(BF16), (r2 < c2).astype(BF16),
                        preferred_element_type=F32)[0:1]
    slot = run_start + rank
    slot1 = jnp.sum(jnp.where(lane == i1, slot, 0.0), axis=-1, keepdims=True)
    slot2 = jnp.sum(jnp.where(lane == i2, slot, 0.0), axis=-1, keepdims=True)
    meta = jnp.where(lane == 0, slot1, 0.0)
    for idx, val in ((1, slot2), (2, gate1), (3, gate2), (4, i1), (5, i2)):
        meta = jnp.where(lane == idx, val, meta)
    meta_ref[...] = meta
    cnt_ref[...] = jnp.broadcast_to(counts.astype(I32)[None], cnt_ref.shape)


def _router_call(y, w_out, x, gain, w_router_stack):
    n = x.shape[0]
    n_tiles = n // ROUTE_TILE
    row = pl.BlockSpec((ROUTE_TILE, D_MODEL), lambda i: (i, 0))
    return pl.pallas_call(
        _router_kernel,
        grid=(n_tiles,),
        in_specs=[row, _const_spec((D_MODEL, D_MODEL)), row, _const_spec((1, D_MODEL)),
                  _const_spec((3 * D_MODEL, LANES))],
        out_specs=[row, row, pl.BlockSpec((ROUTE_TILE, LANES), lambda i: (i, 0)),
                   pl.BlockSpec((1, 8, LANES), lambda i: (i, 0, 0))],
        out_shape=[jax.ShapeDtypeStruct((n, D_MODEL), F32),
                   jax.ShapeDtypeStruct((n, D_MODEL), BF16),
                   jax.ShapeDtypeStruct((n, LANES), F32),
                   jax.ShapeDtypeStruct((n_tiles, 8, LANES), I32)],
        compiler_params=_params("arbitrary"),
        name="moe_router",
    )(y, w_out, x, gain, w_router_stack)


def _run_copies(tbl_ref, tile, stage_ref, hbm_ref, sem, to_hbm):
    base = tile * (3 * N_EXPERTS)
    for e in range(N_EXPERTS):
        lo = tbl_ref[base + e]
        n = tbl_ref[base + N_EXPERTS + e]
        dst = tbl_ref[base + 2 * N_EXPERTS + e]
        for sz in RUN_SIZES:
            done = jnp.bitwise_and(n, ~(2 * sz - 1))
            st = stage_ref.at[pl.ds(pl.multiple_of(lo + done, ROUTE_ALIGN), sz)]
            hb = hbm_ref.at[pl.ds(pl.multiple_of(dst + done, ROUTE_ALIGN), sz)]
            copy = pltpu.make_async_copy(st, hb, sem) if to_hbm else pltpu.make_async_copy(hb, st, sem)
            yield jnp.bitwise_and(n, sz) != 0, copy


def _start_runs(tbl_ref, tile, stage_ref, hbm_ref, sem, to_hbm):
    for cond, copy in _run_copies(tbl_ref, tile, stage_ref, hbm_ref, sem, to_hbm):
        pl.when(cond)(copy.start)


def _wait_runs(tbl_ref, tile, stage_ref, hbm_ref, sem, to_hbm):
    for cond, copy in _run_copies(tbl_ref, tile, stage_ref, hbm_ref, sem, to_hbm):
        pl.when(cond)(copy.wait)


def _dispatch_kernel(tbl_ref, h_ref, meta_ref, xs_ref, stage_ref, sem):
    tile = pl.program_id(0)
    buf = tile % 2
    meta = meta_ref[...]
    slot1 = meta[:, 0:1].astype(I32)
    slot2 = meta[:, 1:2].astype(I32)
    rows = lax.broadcasted_iota(I32, (ROUTE_TILE, ROUTE_ROWS), 1)
    pick = jnp.logical_or(rows == slot1, rows == slot2).astype(BF16)
    stage_ref[buf] = lax.dot_general(pick, h_ref[...], _TN, preferred_element_type=F32).astype(BF16)
    _start_runs(tbl_ref, tile, stage_ref.at[buf], xs_ref, sem.at[buf], True)

    @pl.when(tile > 0)
    def _():
        _wait_runs(tbl_ref, tile - 1, stage_ref.at[1 - buf], xs_ref, sem.at[1 - buf], True)

    @pl.when(tile == pl.num_programs(0) - 1)
    def _():
        _wait_runs(tbl_ref, tile, stage_ref.at[buf], xs_ref, sem.at[buf], True)


def _dispatch_call(tbl, h, meta, sorted_rows):
    n = h.shape[0]
    grid_spec = pltpu.PrefetchScalarGridSpec(
        num_scalar_prefetch=1,
        grid=(n // ROUTE_TILE,),
        in_specs=[pl.BlockSpec((ROUTE_TILE, D_MODEL), lambda i, tbl: (i, 0)),
                  pl.BlockSpec((ROUTE_TILE, LANES), lambda i, tbl: (i, 0))],
        out_specs=pl.BlockSpec(memory_space=pl.ANY),
        scratch_shapes=[pltpu.VMEM((2, ROUTE_ROWS, D_MODEL), BF16), pltpu.SemaphoreType.DMA((2,))],
    )
    return pl.pallas_call(
        _dispatch_kernel,
        grid_spec=grid_spec,
        out_shape=jax.ShapeDtypeStruct((sorted_rows, D_MODEL), BF16),
        compiler_params=_params("arbitrary"),
        name="moe_dispatch",
    )(tbl, h, meta)


def _expert_kernel(te_ref, trow_ref, nval_ref, x_ref, wg_ref, wu_ref, wd_ref, o_ref, act_ref, acc_ref):
    i = pl.program_id(0)
    j = pl.program_id(1)
    nv = nval_ref[i]

    @pl.when(nv > 0)
    def _():
        @pl.when(j == 0)
        def _():
            acc_ref[...] = jnp.zeros_like(acc_ref)

        x = x_ref[...]
        row = lax.broadcasted_iota(I32, x.shape, 0)
        x = jnp.where(row < nv, x, jnp.zeros_like(x))
        acc_ref[...] += _swiglu_block(x, wg_ref, wu_ref, wd_ref, act_ref)

        @pl.when(j == pl.num_programs(1) - 1)
        def _():
            o_ref[...] = acc_ref[...].astype(BF16)


def _expert_call(te, trow, nval, xs, w_gu, w_down, fc, tile_rows):
    n_tiles = te.shape[0]
    d_ff = w_down.shape[1]
    nf = d_ff // fc

    def jj(i, j, nval):
        return jnp.where(nval[i] > 0, j, nf - 1)

    grid_spec = pltpu.PrefetchScalarGridSpec(
        num_scalar_prefetch=3,
        grid=(n_tiles, nf),
        in_specs=[pl.BlockSpec((tile_rows, D_MODEL), lambda i, j, te, trow, nval: (trow[i], 0)),
                  pl.BlockSpec((None, D_MODEL, fc), lambda i, j, te, trow, nval: (te[i], 0, jj(i, j, nval))),
                  pl.BlockSpec((None, D_MODEL, fc),
                               lambda i, j, te, trow, nval: (te[i], 0, jj(i, j, nval) + nf)),
                  pl.BlockSpec((None, fc, D_MODEL), lambda i, j, te, trow, nval: (te[i], jj(i, j, nval), 0))],
        out_specs=pl.BlockSpec((tile_rows, D_MODEL), lambda i, j, te, trow, nval: (trow[i], 0)),
        scratch_shapes=[pltpu.VMEM((tile_rows, fc), BF16), pltpu.VMEM((tile_rows, D_MODEL), F32)],
    )
    return pl.pallas_call(
        _expert_kernel,
        grid_spec=grid_spec,
        out_shape=jax.ShapeDtypeStruct(xs.shape, BF16),
        compiler_params=_params("arbitrary", "arbitrary"),
        name="moe_experts",
    )(te, trow, nval, xs, w_gu, w_gu, w_down)


def _combine_kernel(tbl_ref, ys_ref, meta_ref, x_ref, g_ref, o_ref, stage_ref, sem):
    tile = pl.program_id(0)
    buf = tile % 2

    @pl.when(tile == 0)
    def _():
        stage_ref[...] = jnp.zeros_like(stage_ref)
        _start_runs(tbl_ref, tile, stage_ref.at[buf], ys_ref, sem.at[buf], False)

    @pl.when(tile + 1 < pl.num_programs(0))
    def _():
        _start_runs(tbl_ref, tile + 1, stage_ref.at[1 - buf], ys_ref, sem.at[1 - buf], False)

    _wait_runs(tbl_ref, tile, stage_ref.at[buf], ys_ref, sem.at[buf], False)
    meta = meta_ref[...]
    rows = lax.broadcasted_iota(I32, (ROUTE_TILE, ROUTE_ROWS), 1)
    stage = stage_ref[buf]
    y1 = jnp.dot((rows == meta[:, 0:1].astype(I32)).astype(BF16), stage, preferred_element_type=F32)
    y2 = jnp.dot((rows == meta[:, 1:2].astype(I32)).astype(BF16), stage, preferred_element_type=F32)
    x = x_ref[...] + (meta[:, 2:3] * y1 + meta[:, 3:4] * y2)
    o_ref[...] = _rms(x, g_ref[...])


def _combine_call(tbl, ys, meta, x, gain):
    n = x.shape[0]
    row = pl.BlockSpec((ROUTE_TILE, D_MODEL), lambda i, tbl: (i, 0))
    grid_spec = pltpu.PrefetchScalarGridSpec(
        num_scalar_prefetch=1,
        grid=(n // ROUTE_TILE,),
        in_specs=[pl.BlockSpec(memory_space=pl.ANY),
                  pl.BlockSpec((ROUTE_TILE, LANES), lambda i, tbl: (i, 0)),
                  row,
                  pl.BlockSpec((1, D_MODEL), lambda i, tbl: (0, 0))],
        out_specs=row,
        scratch_shapes=[pltpu.VMEM((2, ROUTE_ROWS, D_MODEL), BF16), pltpu.SemaphoreType.DMA((2,))],
    )
    return pl.pallas_call(
        _combine_kernel,
        grid_spec=grid_spec,
        out_shape=jax.ShapeDtypeStruct((n, D_MODEL), F32),
        compiler_params=_params("arbitrary"),
        name="moe_combine",
    )(tbl, ys, meta, x, gain)


def _routing_tables(counts, n_row_tiles, tile_rows):
    padded = (counts + (ROUTE_ALIGN - 1)) // ROUTE_ALIGN * ROUTE_ALIGN
    stage_off = jnp.cumsum(padded, axis=1) - padded
    total = jnp.sum(padded, axis=0)
    tiles_e = (total + (tile_rows - 1)) // tile_rows
    region = jnp.cumsum(tiles_e) - tiles_e
    dst = region[None, :] * tile_rows + jnp.cumsum(padded, axis=0) - padded
    tbl = jnp.concatenate([stage_off, padded, dst], axis=1).reshape(-1).astype(I32)

    used = jnp.sum(tiles_e)
    ids = jnp.arange(n_row_tiles, dtype=I32)
    ends = jnp.cumsum(tiles_e)
    te = jnp.sum((ids[:, None] >= ends[None, :]).astype(I32), axis=1)
    te = jnp.minimum(te, N_EXPERTS - 1)
    nval = jnp.clip(total[te] - (ids - region[te]) * tile_rows, 0, tile_rows)
    valid = ids < used
    nval = jnp.where(valid, nval, 0).astype(I32)
    last = jnp.maximum(used - 1, 0)
    te = jnp.where(valid, te, te[last]).astype(I32)
    trow = jnp.where(valid, ids, n_row_tiles).astype(I32)
    return tbl, te, trow, nval


def _moe_final(y, w_out, x, gain, w_router_stack, w_gu, w_down, final_gain, fc):
    n = x.shape[0]
    n_tiles = n // ROUTE_TILE
    tile_rows = min(EXPERT_TILE, -(-2 * n // N_EXPERTS // MXU_COLS) * MXU_COLS)
    max_rows = 2 * n + n_tiles * N_EXPERTS * (ROUTE_ALIGN - 1)
    n_row_tiles = -(-max_rows // tile_rows) + N_EXPERTS
    x, h, meta, cnt = _router_call(y, w_out, x, gain, w_router_stack)
    tbl, te, trow, nval = _routing_tables(cnt[:, 0, :N_EXPERTS], n_row_tiles, tile_rows)
    xs = _dispatch_call(tbl, h, meta, (n_row_tiles + 1) * tile_rows)
    ys = _expert_call(te, trow, nval, xs, w_gu, w_down, fc, tile_rows)
    return _combine_call(tbl, ys, meta, x, final_gain)


def _stream(x, cache_k, cache_v, s0_t, lb1, weights, tm, chunk, sub):
    (norm_mix, norm_ch, norm_final, w_qkv, w_o, rel_table, w_in, w_out, out_norm,
     w_gu, w_down, w_router, w_egu, w_edown) = weights
    batch, seq, _ = x.shape
    n = batch * seq
    x = x.reshape(n, D_MODEL)
    row = lambda a: a.reshape(1, D_MODEL)

    if cache_k is None:
        q, k, v, kf, vf = _qkv_call(x, row(norm_mix[0]), w_qkv, seq // tm, tm)
        att = _band_attn_call(q, k, v, _band_bias(rel_table), batch, seq)
        keep = min(WINDOW, seq)
        new_k = kf.reshape(batch, keep, ATT_HEADS, ATT_HEAD_DIM)
        new_v = vf.reshape(batch, keep, ATT_HEADS, ATT_HEAD_DIM)
    else:
        q, k, v, kf, vf = _qkv_call(x, row(norm_mix[0]), w_qkv, 1, tm)
        w = cache_k.shape[1]
        att = _cache_attn_call(q, k, v, cache_k.reshape(batch, w, D_MODEL), cache_v.reshape(batch, w, D_MODEL),
                               _cache_bias(rel_table, w, seq), batch, seq)
        new_k = kf.reshape(batch, seq, ATT_HEADS, ATT_HEAD_DIM)
        new_v = vf.reshape(batch, seq, ATT_HEADS, ATT_HEAD_DIM)
    x = _proj_residual_call(att, w_o, x, tm)
    x = _ffn_call(x, row(norm_ch[0]), w_gu, w_down, min(n, 2 * tm), w_down.shape[0] // 2)

    q, f, i, z = _hg_in_call(x, row(norm_mix[1]), w_in, tm)
    y, s_t = _hg_scan_call(q, f, i, z, lb1, row(out_norm), s0_t, batch, seq, chunk, sub)
    y = _moe_final(y, w_out, x, row(norm_ch[1]), w_router, w_egu, w_edown, row(norm_final),
                   w_edown.shape[1] // 2)
    return y.reshape(batch, seq, D_MODEL), new_k, new_v, jnp.swapaxes(s_t, -1, -2)


def kernel(x_prompt, x_sample, cache_k, cache_v, state_hgrn, norm_mix, norm_ch, norm_final, att_w_qkv, att_w_o, att_rel_bias, hg_w_in, hg_w_out, hg_out_norm, hg_lower_bounds, ffn_w_gu, ffn_w_down, moe_w_router, moe_w_gu, moe_w_down):
    lb = jnp.cumsum(jax.nn.softmax(hg_lower_bounds.astype(F32), axis=0), axis=0)
    lb1 = (lb[1] - lb[0]).reshape(1, D_MODEL)
    w_router = jnp.pad(moe_w_router[0].astype(F32), ((0, 0), (0, LANES - N_EXPERTS)))
    w_router_hi = w_router.astype(BF16)
    w_router_lo = (w_router - w_router_hi.astype(F32)).astype(BF16)
    w_router = jnp.concatenate([w_router_hi, w_router_hi, w_router_lo], axis=0)
    weights = (norm_mix.astype(F32), norm_ch.astype(F32), norm_final.astype(F32),
               att_w_qkv[0].astype(BF16), att_w_o[0].astype(BF16), att_rel_bias[0],
               hg_w_in[0].astype(BF16), hg_w_out[0].astype(BF16), hg_out_norm[0].astype(F32),
               ffn_w_gu[0].astype(BF16), ffn_w_down[0].astype(BF16),
               w_router, moe_w_gu[0].astype(BF16), moe_w_down[0].astype(BF16))

    yp, kp, vp, sp = _stream(x_prompt, None, None, None, lb1, weights, 512, 128, 32)
    s0_t = jnp.swapaxes(state_hgrn[0].astype(F32), -1, -2)
    dec_tokens = x_sample.shape[0] * x_sample.shape[1]
    ys, ks, vs, ss = _stream(x_sample, cache_k[0], cache_v[0], s0_t, lb1, weights,
                             dec_tokens, x_sample.shape[1], x_sample.shape[1])
    return (yp, ys, kp[None], vp[None], ks[None], vs[None], sp[None], ss[None])
```

```python
import functools

import jax
import jax.numpy as jnp
import numpy as np
from jax import lax
from jax.experimental import pallas as pl
from jax.experimental.pallas import tpu as pltpu

F32 = jnp.float32
BF16 = jnp.bfloat16
I32 = jnp.int32

D_MODEL = 1024
ATT_HEADS = 16
ATT_HEAD_DIM = 64
CHUNK = 64
LEFT_CHUNKS = 8
WINDOW = LEFT_CHUNKS * CHUNK
REL_CLIP = 128
HG_HEADS = 8
HG_EXPAND = 128
N_EXPERTS = 8
RMS_EPS = 1e-6

LANES = 128
BF16_SUBLANES = 16
MXU_COLS = 256
VMEM_LIMIT = 56 * 1024 * 1024

HEAD_PAIRS = ATT_HEADS // 2
Q_GROUP = 2 * WINDOW
Q_HALF = WINDOW // 2
K_SPAN = Q_HALF + WINDOW
PAIRS_PER_STEP = 2
MASKED = -1e30

ROUTE_TILE = 512
ROUTE_ALIGN = BF16_SUBLANES
ROUTE_ROWS = 2 * ROUTE_TILE + N_EXPERTS * (ROUTE_ALIGN - 1)
ROUTE_ROWS = -(-ROUTE_ROWS // LANES) * LANES
RUN_SIZES = tuple(ROUTE_TILE >> s for s in range(6))
EXPERT_TILE = 1024

SCAN_CHUNKS_PER_STEP = 4
ROUTER_PARTS = 4

_NT = (((1,), (1,)), ((), ()))
_TN = (((0,), (0,)), ((), ()))


def _params(*sem):
    return pltpu.CompilerParams(dimension_semantics=sem, vmem_limit_bytes=VMEM_LIMIT)


def _rms(x, gain):
    return x * lax.rsqrt(jnp.mean(x * x, axis=-1, keepdims=True) + RMS_EPS) * gain


def _sigmoid(x):
    return 1.0 / (1.0 + jnp.exp(-x))


def _const_spec(shape):
    zeros = (0,) * len(shape)
    return pl.BlockSpec(shape, lambda *_: zeros)


def _qkv_kernel(x_ref, g_ref, w_ref, q_ref, k_ref, v_ref, kf_ref, vf_ref, *, tiles_per_seq):
    h = _rms(x_ref[...], g_ref[...]).astype(BF16)
    d = D_MODEL
    q = jnp.dot(h, w_ref[:, 0:d], preferred_element_type=F32) * (ATT_HEAD_DIM ** -0.5)
    k = jnp.dot(h, w_ref[:, d:2 * d], preferred_element_type=F32)
    v = jnp.dot(h, w_ref[:, 2 * d:3 * d], preferred_element_type=F32)
    for p in range(HEAD_PAIRS):
        sl = slice(p * LANES, (p + 1) * LANES)
        q_ref[p] = q[:, sl].astype(BF16)
        k_ref[p] = k[:, sl].astype(BF16)
        v_ref[p] = v[:, sl].astype(BF16)

    @pl.when(pl.program_id(0) % tiles_per_seq == tiles_per_seq - 1)
    def _():
        kf_ref[...] = k
        vf_ref[...] = v


def _qkv_call(x, gain, w, tiles_per_seq, tm):
    n = x.shape[0]
    n_tiles = n // tm
    n_seq = n_tiles // tiles_per_seq
    slab = jax.ShapeDtypeStruct((HEAD_PAIRS, n, LANES), BF16)
    tail = jax.ShapeDtypeStruct((n_seq * tm, D_MODEL), F32)
    slab_spec = pl.BlockSpec((HEAD_PAIRS, tm, LANES), lambda i: (0, i, 0))
    tail_spec = pl.BlockSpec((tm, D_MODEL), lambda i: (i // tiles_per_seq, 0))
    return pl.pallas_call(
        functools.partial(_qkv_kernel, tiles_per_seq=tiles_per_seq),
        grid=(n_tiles,),
        in_specs=[pl.BlockSpec((tm, D_MODEL), lambda i: (i, 0)),
                  _const_spec((1, D_MODEL)),
                  _const_spec((D_MODEL, 3 * D_MODEL))],
        out_specs=[slab_spec, slab_spec, slab_spec, tail_spec, tail_spec],
        out_shape=[slab, slab, slab, tail, tail],
        compiler_params=_params("arbitrary"),
        name="qkv_proj",
    )(x, gain, w)


def _band_attn_kernel(q_ref, kp_ref, kc_ref, vp_ref, vc_ref, bias_ref, o_ref):
    first_group = pl.program_id(2) == 0
    lane = lax.broadcasted_iota(I32, (2 * Q_HALF, LANES), 1)
    row = lax.broadcasted_iota(I32, (2 * Q_HALF, LANES), 0)
    own = (lane < ATT_HEAD_DIM) == (row < Q_HALF)
    first_head = lax.broadcasted_iota(I32, (Q_HALF, LANES), 1) < ATT_HEAD_DIM
    def window(prev_ref, cur_ref, pr, lo):
        hi = lo + K_SPAN
        if lo >= WINDOW:
            return cur_ref[pr, lo - WINDOW:hi - WINDOW, :]
        return jnp.concatenate([prev_ref[pr, lo:, :], cur_ref[pr, 0:hi - WINDOW, :]], axis=0)

    for pr in range(PAIRS_PER_STEP):
        for part in range(q_ref.shape[1] // Q_HALF):
            rows = slice(part * Q_HALF, (part + 1) * Q_HALF)
            qh = q_ref[pr, rows, :]
            q2 = jnp.concatenate([qh, qh], axis=0)
            q2 = jnp.where(own, q2, jnp.zeros_like(q2))
            keys = window(kp_ref, kc_ref, pr, part * Q_HALF)
            vals = window(vp_ref, vc_ref, pr, part * Q_HALF)
            s = lax.dot_general(q2, keys, _NT, preferred_element_type=F32)
            if (part + 1) * Q_HALF <= WINDOW:
                s = s + bias_ref[pr, jnp.where(first_group, 1 + part, 0)]
            else:
                s = s + bias_ref[pr, 0]
            m = jnp.max(s, axis=-1, keepdims=True)
            p = jnp.exp(s - m)
            l = jnp.sum(p, axis=-1, keepdims=True)
            o2 = jnp.dot(p.astype(BF16), vals, preferred_element_type=F32) / l
            o = jnp.where(first_head, o2[:Q_HALF], o2[Q_HALF:])
            o_ref[rows, pr * LANES:(pr + 1) * LANES] = o.astype(BF16)


def _band_bias(table):
    qi = np.arange(Q_HALF)[:, None]
    kj = np.arange(K_SPAN)[None, :]
    cq, ck = qi // CHUNK, kj // CHUNK
    band = (ck >= cq) & (ck <= cq + LEFT_CHUNKS)
    valid = np.stack([band, band & (kj >= WINDOW), band & (kj >= WINDOW - Q_HALF)])
    table = table.astype(F32)
    n_rel = table.shape[1]
    span = Q_HALF + K_SPAN - 1
    lo_rep = (K_SPAN - 1 - WINDOW) - (CHUNK - 1)
    hi_rep = span - lo_rep - n_rel
    line = jnp.concatenate([jnp.repeat(table[:, :1], lo_rep, axis=1), table,
                            jnp.repeat(table[:, -1:], hi_rep, axis=1)], axis=1)[:, ::-1]
    flow = jnp.tile(line, (1, Q_HALF + 1))[:, Q_HALF - 1:Q_HALF - 1 + Q_HALF * (span - 1)]
    b = flow.reshape(-1, Q_HALF, span - 1)[:, :, :K_SPAN]
    b = jnp.where(valid[None], b[:, None], MASKED).reshape(HEAD_PAIRS, 2, 3, Q_HALF, K_SPAN)
    return jnp.swapaxes(b, 1, 2).reshape(HEAD_PAIRS, 3, 2 * Q_HALF, K_SPAN)


def _band_attn_call(q, k, v, bias, batch, seq):
    groups = seq // Q_GROUP
    n = batch * seq

    def cur(p, b, g):
        return (p, b * groups + g, 0)

    def prev(p, b, g):
        return (p, (b * groups + g) * (Q_GROUP // WINDOW) - jnp.minimum(g, 1), 0)

    blk = (PAIRS_PER_STEP, Q_GROUP, LANES)
    blk_prev = (PAIRS_PER_STEP, WINDOW, LANES)
    return pl.pallas_call(
        _band_attn_kernel,
        grid=(HEAD_PAIRS // PAIRS_PER_STEP, batch, groups),
        in_specs=[pl.BlockSpec(blk, cur), pl.BlockSpec(blk_prev, prev), pl.BlockSpec(blk, cur),
                  pl.BlockSpec(blk_prev, prev), pl.BlockSpec(blk, cur),
                  pl.BlockSpec((PAIRS_PER_STEP, 3, 2 * Q_HALF, K_SPAN), lambda p, b, g: (p, 0, 0, 0))],
        out_specs=pl.BlockSpec((Q_GROUP, PAIRS_PER_STEP * LANES), lambda p, b, g: (b * groups + g, p)),
        out_shape=jax.ShapeDtypeStruct((n, D_MODEL), BF16),
        compiler_params=_params("arbitrary", "arbitrary", "arbitrary"),
        name="band_attention",
    )(q, k, k, v, v, bias)


def _cache_attn_kernel(q_ref, k_ref, v_ref, ck_ref, cv_ref, bias_ref, o_ref):
    s_len = q_ref.shape[1]
    lane = lax.broadcasted_iota(I32, (s_len, LANES), 1)
    low = lane < ATT_HEAD_DIM
    scores = []
    for p in range(HEAD_PAIRS):
        sl = slice(p * LANES, (p + 1) * LANES)
        keys = jnp.concatenate([ck_ref[:, sl].astype(BF16), k_ref[p]], axis=0)
        qp = q_ref[p]
        for hd in range(2):
            qm = jnp.where(low if hd == 0 else jnp.logical_not(low), qp, jnp.zeros_like(qp))
            scores.append(lax.dot_general(qm, keys, _NT, preferred_element_type=F32) + bias_ref[2 * p + hd])
    probs = []
    for s in scores:
        m = jnp.max(s, axis=-1, keepdims=True)
        e = jnp.exp(s - m)
        probs.append((e.astype(BF16), jnp.sum(e, axis=-1, keepdims=True)))
    for p in range(HEAD_PAIRS):
        sl = slice(p * LANES, (p + 1) * LANES)
        vals = jnp.concatenate([cv_ref[:, sl].astype(BF16), v_ref[p]], axis=0)
        outs = [jnp.dot(e, vals, preferred_element_type=F32) / l for e, l in probs[2 * p:2 * p + 2]]
        o_ref[:, sl] = jnp.where(low, outs[0], outs[1]).astype(BF16)


def _cache_bias(table, w, s_len):
    qi = np.arange(s_len)[:, None]
    kj = np.arange(w + s_len)[None, :]
    idx = np.clip(w + qi - kj, -(CHUNK - 1), REL_CLIP) + (CHUNK - 1)
    return table.astype(F32)[:, idx]


def _cache_attn_call(q, k, v, cache_k, cache_v, bias, batch, s_len):
    w = cache_k.shape[1]
    slab = pl.BlockSpec((HEAD_PAIRS, s_len, LANES), lambda b: (0, b, 0))
    cache = pl.BlockSpec((None, w, D_MODEL), lambda b: (b, 0, 0))
    return pl.pallas_call(
        _cache_attn_kernel,
        grid=(batch,),
        in_specs=[slab, slab, slab, cache, cache, _const_spec((ATT_HEADS, s_len, w + s_len))],
        out_specs=pl.BlockSpec((s_len, D_MODEL), lambda b: (b, 0)),
        out_shape=jax.ShapeDtypeStruct((batch * s_len, D_MODEL), BF16),
        compiler_params=_params("arbitrary"),
        name="cache_attention",
    )(q, k, v, cache_k, cache_v, bias)


def _swiglu_block(h, wg_ref, wu_ref, wd_ref, act_ref):
    fc = act_ref.shape[1]
    for c0 in range(0, fc, MXU_COLS):
        sl = slice(c0, min(c0 + MXU_COLS, fc))
        gate = jnp.dot(h, wg_ref[:, sl], preferred_element_type=F32)
        up = jnp.dot(h, wu_ref[:, sl], preferred_element_type=F32)
        act_ref[:, sl] = (gate * _sigmoid(gate) * up).astype(BF16)
    return jnp.dot(act_ref[...], wd_ref[...], preferred_element_type=F32)


def _ffn_kernel(a_ref, wo_ref, x_ref, g_ref, wg_ref, wu_ref, wd_ref, o_ref, h_ref, act_ref):
    j = pl.program_id(1)

    @pl.when(j == 0)
    def _():
        x1 = x_ref[...] + jnp.dot(a_ref[...], wo_ref[...], preferred_element_type=F32)
        o_ref[...] = x1
        h_ref[...] = _rms(x1, g_ref[...]).astype(BF16)

    o_ref[...] += _swiglu_block(h_ref[...], wg_ref, wu_ref, wd_ref, act_ref)


def _ffn_call(a, w_o, x, gain, w_gu, w_down, tm, fc):
    n = x.shape[0]
    d_ff = w_down.shape[0]
    nf = d_ff // fc
    row = pl.BlockSpec((tm, D_MODEL), lambda i, j: (i, 0))
    return pl.pallas_call(
        _ffn_kernel,
        grid=(n // tm, nf),
        in_specs=[row, _const_spec((D_MODEL, D_MODEL)), row, _const_spec((1, D_MODEL)),
                  pl.BlockSpec((D_MODEL, fc), lambda i, j: (0, j)),
                  pl.BlockSpec((D_MODEL, fc), lambda i, j: (0, j + nf)),
                  pl.BlockSpec((fc, D_MODEL), lambda i, j: (j, 0))],
        out_specs=row,
        out_shape=jax.ShapeDtypeStruct((n, D_MODEL), F32),
        scratch_shapes=[pltpu.VMEM((tm, D_MODEL), BF16), pltpu.VMEM((tm, fc), BF16)],
        compiler_params=_params("arbitrary", "arbitrary"),
        name="swiglu_ffn",
    )(a, w_o, x, gain, w_gu, w_gu, w_down)


def _hg_in_kernel(x_ref, g_ref, w_ref, q_ref, f_ref, i_ref, z_ref):
    h = _rms(x_ref[...], g_ref[...]).astype(BF16)
    d = D_MODEL
    for idx, ref in enumerate((q_ref, f_ref, i_ref, z_ref)):
        ref[...] = jnp.dot(h, w_ref[:, idx * d:(idx + 1) * d], preferred_element_type=F32)


def _hg_in_call(x, gain, w, tm):
    n = x.shape[0]
    row = pl.BlockSpec((tm, D_MODEL), lambda i: (i, 0))
    out = jax.ShapeDtypeStruct((n, D_MODEL), F32)
    return pl.pallas_call(
        _hg_in_kernel,
        grid=(n // tm,),
        in_specs=[row, _const_spec((1, D_MODEL)), _const_spec((D_MODEL, 4 * D_MODEL))],
        out_specs=[row, row, row, row],
        out_shape=[out, out, out, out],
        compiler_params=_params("arbitrary"),
        name="hgrn_in_proj",
    )(x, gain, w)


def _hg_scan_kernel(*refs, chunk, sub, has_s0):
    st_refs = refs[-HG_HEADS:]
    refs = refs[:-HG_HEADS]
    if has_s0:
        q_ref, f_ref, i_ref, z_ref, lb_ref, on_ref, s0_ref, y_ref, sf_ref = refs
    else:
        q_ref, f_ref, i_ref, z_ref, lb_ref, on_ref, y_ref, sf_ref = refs
    t = pl.program_id(1)

    @pl.when(t == 0)
    def _():
        for hd in range(HG_HEADS):
            st_refs[hd][...] = s0_ref[hd] if has_s0 else jnp.zeros((HG_EXPAND, HG_EXPAND), F32)

    lb = lb_ref[...]
    r = lax.broadcasted_iota(I32, (chunk, chunk), 0)
    c = lax.broadcasted_iota(I32, (chunk, chunk), 1)
    causal = c <= r
    tri = causal.astype(BF16)
    n_sub = chunk // sub
    for c0 in range(0, q_ref.shape[0], chunk):
        _hg_scan_chunk(slice(c0, c0 + chunk), q_ref, f_ref, i_ref, z_ref, on_ref, y_ref, st_refs,
                       lb, r, causal, tri, chunk, sub, n_sub)

    @pl.when(t == pl.num_programs(1) - 1)
    def _():
        for hd in range(HG_HEADS):
            sf_ref[hd] = st_refs[hd][...]


def _hg_scan_chunk(rows, q_ref, f_ref, i_ref, z_ref, on_ref, y_ref, st_refs,
                   lb, r, causal, tri, chunk, sub, n_sub):
    forget = lb + (1.0 - lb) * _sigmoid(f_ref[rows, :])
    log_f = jnp.log(forget)
    key_all = 1.0 - forget
    hi = log_f.astype(BF16)
    rest = log_f - hi.astype(F32)
    mid = rest.astype(BF16)
    lo = (rest - mid.astype(F32)).astype(BF16)
    g_all = jnp.dot(jnp.concatenate([tri, tri, tri], axis=1), jnp.concatenate([hi, mid, lo], axis=0),
                    preferred_element_type=F32)

    heads = [slice(hd * HG_EXPAND, (hd + 1) * HG_EXPAND) for hd in range(HG_HEADS)]
    values, inter, scores = [], [], []
    for hd, sl in enumerate(heads):
        g = g_all[:, sl]
        q = q_ref[rows, sl] * (HG_EXPAND ** -0.5)
        k = key_all[:, sl]
        v = i_ref[rows, sl].astype(BF16)
        g_last = g[chunk - 1:chunk, :]
        state = st_refs[hd][...]
        q_in = (q * jnp.exp(g)).astype(BF16)
        k_end = (k * jnp.exp(g_last - g)).astype(BF16)
        o_inter = lax.dot_general(q_in, state.astype(BF16), _NT, preferred_element_type=F32)
        st_refs[hd][...] = state * jnp.exp(g_last) + lax.dot_general(v, k_end, _TN, preferred_element_type=F32)

        g_mids = [g[b * sub + sub // 2 - 1:b * sub + sub // 2, :] for b in range(n_sub)]
        g_mid_rows = jnp.concatenate([jnp.broadcast_to(m, (sub, HG_EXPAND)) for m in g_mids], axis=0)
        q_sub = (q * jnp.exp(g - g_mid_rows)).astype(BF16)
        k_blocks = []
        scaled = None
        for b in range(n_sub):
            blk = slice(b * sub, (b + 1) * sub)
            fresh = k[blk] * jnp.exp(g_mids[b] - g[blk])
            if b == 0:
                scaled = fresh
            else:
                scaled = jnp.concatenate([scaled * jnp.exp(g_mids[b] - g_mids[b - 1]), fresh], axis=0)
            k_blocks.append(scaled.astype(BF16))
            if (b + 1) * sub < chunk:
                k_blocks.append(jnp.zeros((chunk - (b + 1) * sub, HG_EXPAND), BF16))
        k_sub = k_blocks[0] if len(k_blocks) == 1 else jnp.concatenate(k_blocks, axis=0)
        values.append(v)
        inter.append(o_inter)
        scores.append(lax.dot_general(q_sub, k_sub, _NT, preferred_element_type=F32))

    outs = []
    for v, o_inter, a_all in zip(values, inter, scores):
        a = a_all[:, 0:chunk]
        for b in range(1, n_sub):
            a = jnp.where(r >= b * sub, a_all[:, b * chunk:(b + 1) * chunk], a)
        a = jnp.where(causal, a, 0.0).astype(BF16)
        outs.append(o_inter + jnp.dot(a, v, preferred_element_type=F32))
    for sl, o in zip(heads, outs):
        o = o * lax.rsqrt(jnp.mean(o * o, axis=-1, keepdims=True) + RMS_EPS) * on_ref[:, sl]
        z = z_ref[rows, sl]
        y_ref[rows, sl] = (o * (z * _sigmoid(z))).astype(BF16)


def _hg_scan_call(q, f, i, z, lb, out_norm, s0_t, batch, seq, chunk, sub):
    step_rows = min(seq, SCAN_CHUNKS_PER_STEP * chunk)
    n_chunks = seq // step_rows
    has_s0 = s0_t is not None
    row = pl.BlockSpec((step_rows, D_MODEL), lambda b, t: (b * n_chunks + t, 0))
    st = pl.BlockSpec((None, HG_HEADS, HG_EXPAND, HG_EXPAND), lambda b, t: (b, 0, 0, 0))
    in_specs = [row, row, row, row, _const_spec((1, D_MODEL)), _const_spec((1, D_MODEL))]
    args = [q, f, i, z, lb, out_norm]
    if has_s0:
        in_specs.append(st)
        args.append(s0_t)
    return pl.pallas_call(
        functools.partial(_hg_scan_kernel, chunk=chunk, sub=sub, has_s0=has_s0),
        grid=(batch, n_chunks),
        in_specs=in_specs,
        out_specs=[row, st],
        out_shape=[jax.ShapeDtypeStruct((batch * seq, D_MODEL), BF16),
                   jax.ShapeDtypeStruct((batch, HG_HEADS, HG_EXPAND, HG_EXPAND), F32)],
        scratch_shapes=[pltpu.VMEM((HG_EXPAND, HG_EXPAND), F32)] * HG_HEADS,
        compiler_params=_params("arbitrary", "arbitrary"),
        name="hgrn_scan",
    )(*args)


def _router_kernel(y_ref, wo_ref, x_ref, g_ref, wr_ref, xo_ref, h_ref, meta_ref, cnt_ref):
    tr = x_ref.shape[0]
    parts = [slice(p * (tr // ROUTER_PARTS), (p + 1) * (tr // ROUTER_PARTS)) for p in range(ROUTER_PARTS)]
    xs = [x_ref[p, :] + jnp.dot(y_ref[p, :], wo_ref[...], preferred_element_type=F32) for p in parts]
    for p, x in zip(parts, xs):
        xo_ref[p, :] = x
    hfs = [_rms(x, g_ref[...]) for x in xs]
    logit_parts = []
    for p, hf in zip(parts, hfs):
        hi = hf.astype(BF16)
        lo = (hf - hi.astype(F32)).astype(BF16)
        h_ref[p, :] = hi
        logit_parts.append(jnp.dot(jnp.concatenate([hi, lo, hi], axis=1), wr_ref[...],
                                   preferred_element_type=F32))
    logits = jnp.concatenate(logit_parts, axis=0)
    lane = lax.broadcasted_iota(I32, (tr, LANES), 1).astype(F32)
    lg = jnp.where(lane < N_EXPERTS, logits, -jnp.inf)
    m1 = jnp.max(lg, axis=-1, keepdims=True)
    i1 = jnp.min(jnp.where(lg == m1, lane, float(LANES)), axis=-1, keepdims=True)
    lg2 = jnp.where(lane == i1, -jnp.inf, lg)
    m2 = jnp.max(lg2, axis=-1, keepdims=True)
    i2 = jnp.min(jnp.where(lg2 == m2, lane, float(LANES)), axis=-1, keepdims=True)
    e = jnp.exp(m2 - m1)
    gate1 = 1.0 / (1.0 + e)
    gate2 = e / (1.0 + e)

    chosen = jnp.logical_or(lane == i1, lane == i2)
    r = lax.broadcasted_iota(I32, (tr, tr), 0)
    c = lax.broadcasted_iota(I32, (tr, tr), 1)
    before = (c < r).astype(BF16)
    rank = jnp.dot(before, chosen.astype(BF16), preferred_element_type=F32)
    counts = jnp.sum(chosen.astype(F32), axis=0, keepdims=True)
    padded = jnp.floor((counts + (ROUTE_ALIGN - 1)) * (1.0 / ROUTE_ALIGN)) * ROUTE_ALIGN
    r2 = lax.broadcasted_iota(I32, (LANES, LANES), 0)
    c2 = lax.broadcasted_iota(I32, (LANES, LANES), 1)
    run_start = jnp.dot(jnp.broadcast_to(padded, (8, LANES)).astype(BF16), (r2 < c2).astype(BF16),
                        preferred_element_type=F32)[0:1]
    slot = run_start + rank
    slot1 = jnp.sum(jnp.where(lane == i1, slot, 0.0), axis=-1, keepdims=True)
    slot2 = jnp.sum(jnp.where(lane == i2, slot, 0.0), axis=-1, keepdims=True)
    meta = jnp.where(lane == 0, slot1, 0.0)
    for idx, val in ((1, slot2), (2, gate1), (3, gate2), (4, i1), (5, i2)):
        meta = jnp.where(lane == idx, val, meta)
    meta_ref[...] = meta
    cnt_ref[...] = jnp.broadcast_to(counts.astype(I32)[None], cnt_ref.shape)


def _router_call(y, w_out, x, gain, w_router_stack):
    n = x.shape[0]
    n_tiles = n // ROUTE_TILE
    row = pl.BlockSpec((ROUTE_TILE, D_MODEL), lambda i: (i, 0))
    return pl.pallas_call(
        _router_kernel,
        grid=(n_tiles,),
        in_specs=[row, _const_spec((D_MODEL, D_MODEL)), row, _const_spec((1, D_MODEL)),
                  _const_spec((3 * D_MODEL, LANES))],
        out_specs=[row, row, pl.BlockSpec((ROUTE_TILE, LANES), lambda i: (i, 0)),
                   pl.BlockSpec((1, 8, LANES), lambda i: (i, 0, 0))],
        out_shape=[jax.ShapeDtypeStruct((n, D_MODEL), F32),
                   jax.ShapeDtypeStruct((n, D_MODEL), BF16),
                   jax.ShapeDtypeStruct((n, LANES), F32),
                   jax.ShapeDtypeStruct((n_tiles, 8, LANES), I32)],
        compiler_params=_params("arbitrary"),
        name="moe_router",
    )(y, w_out, x, gain, w_router_stack)


def _run_copies(tbl_ref, tile, stage_ref, hbm_ref, sem, to_hbm):
    base = tile * (3 * N_EXPERTS)
    for e in range(N_EXPERTS):
        lo = tbl_ref[base + e]
        n = tbl_ref[base + N_EXPERTS + e]
        dst = tbl_ref[base + 2 * N_EXPERTS + e]
        for sz in RUN_SIZES:
            done = jnp.bitwise_and(n, ~(2 * sz - 1))
            st = stage_ref.at[pl.ds(pl.multiple_of(lo + done, ROUTE_ALIGN), sz)]
            hb = hbm_ref.at[pl.ds(pl.multiple_of(dst + done, ROUTE_ALIGN), sz)]
            copy = pltpu.make_async_copy(st, hb, sem) if to_hbm else pltpu.make_async_copy(hb, st, sem)
            yield jnp.bitwise_and(n, sz) != 0, copy


def _start_runs(tbl_ref, tile, stage_ref, hbm_ref, sem, to_hbm):
    for cond, copy in _run_copies(tbl_ref, tile, stage_ref, hbm_ref, sem, to_hbm):
        pl.when(cond)(copy.start)


def _wait_runs(tbl_ref, tile, stage_ref, hbm_ref, sem, to_hbm):
    for cond, copy in _run_copies(tbl_ref, tile, stage_ref, hbm_ref, sem, to_hbm):
        pl.when(cond)(copy.wait)


def _dispatch_kernel(tbl_ref, h_ref, meta_ref, xs_ref, stage_ref, sem):
    tile = pl.program_id(0)
    buf = tile % 2
    meta = meta_ref[...]
    slot1 = meta[:, 0:1].astype(I32)
    slot2 = meta[:, 1:2].astype(I32)
    rows = lax.broadcasted_iota(I32, (ROUTE_TILE, ROUTE_ROWS), 1)
    pick = jnp.logical_or(rows == slot1, rows == slot2).astype(BF16)
    stage_ref[buf] = lax.dot_general(pick, h_ref[...], _TN, preferred_element_type=F32).astype(BF16)
    _start_runs(tbl_ref, tile, stage_ref.at[buf], xs_ref, sem.at[buf], True)

    @pl.when(tile > 0)
    def _():
        _wait_runs(tbl_ref, tile - 1, stage_ref.at[1 - buf], xs_ref, sem.at[1 - buf], True)

    @pl.when(tile == pl.num_programs(0) - 1)
    def _():
        _wait_runs(tbl_ref, tile, stage_ref.at[buf], xs_ref, sem.at[buf], True)


def _dispatch_call(tbl, h, meta, sorted_rows):
    n = h.shape[0]
    grid_spec = pltpu.PrefetchScalarGridSpec(
        num_scalar_prefetch=1,
        grid=(n // ROUTE_TILE,),
        in_specs=[pl.BlockSpec((ROUTE_TILE, D_MODEL), lambda i, tbl: (i, 0)),
                  pl.BlockSpec((ROUTE_TILE, LANES), lambda i, tbl: (i, 0))],
        out_specs=pl.BlockSpec(memory_space=pl.ANY),
        scratch_shapes=[pltpu.VMEM((2, ROUTE_ROWS, D_MODEL), BF16), pltpu.SemaphoreType.DMA((2,))],
    )
    return pl.pallas_call(
        _dispatch_kernel,
        grid_spec=grid_spec,
        out_shape=jax.ShapeDtypeStruct((sorted_rows, D_MODEL), BF16),
        compiler_params=_params("arbitrary"),
        name="moe_dispatch",
    )(tbl, h, meta)


def _expert_kernel(te_ref, trow_ref, nval_ref, x_ref, wg_ref, wu_ref, wd_ref, o_ref, act_ref, acc_ref):
    i = pl.program_id(0)
    j = pl.program_id(1)
    nv = nval_ref[i]

    @pl.when(nv > 0)
    def _():
        @pl.when(j == 0)
        def _():
            acc_ref[...] = jnp.zeros_like(acc_ref)

        x = x_ref[...]
        row = lax.broadcasted_iota(I32, x.shape, 0)
        x = jnp.where(row < nv, x, jnp.zeros_like(x))
        acc_ref[...] += _swiglu_block(x, wg_ref, wu_ref, wd_ref, act_ref)

        @pl.when(j == pl.num_programs(1) - 1)
        def _():
            o_ref[...] = acc_ref[...].astype(BF16)


def _expert_call(te, trow, nval, xs, w_gu, w_down, fc, tile_rows):
    n_tiles = te.shape[0]
    d_ff = w_down.shape[1]
    nf = d_ff // fc

    def jj(i, j, nval):
        return jnp.where(nval[i] > 0, j, nf - 1)

    grid_spec = pltpu.PrefetchScalarGridSpec(
        num_scalar_prefetch=3,
        grid=(n_tiles, nf),
        in_specs=[pl.BlockSpec((tile_rows, D_MODEL), lambda i, j, te, trow, nval: (trow[i], 0)),
                  pl.BlockSpec((None, D_MODEL, fc), lambda i, j, te, trow, nval: (te[i], 0, jj(i, j, nval))),
                  pl.BlockSpec((None, D_MODEL, fc),
                               lambda i, j, te, trow, nval: (te[i], 0, jj(i, j, nval) + nf)),
                  pl.BlockSpec((None, fc, D_MODEL), lambda i, j, te, trow, nval: (te[i], jj(i, j, nval), 0))],
        out_specs=pl.BlockSpec((tile_rows, D_MODEL), lambda i, j, te, trow, nval: (trow[i], 0)),
        scratch_shapes=[pltpu.VMEM((tile_rows, fc), BF16), pltpu.VMEM((tile_rows, D_MODEL), F32)],
    )
    return pl.pallas_call(
        _expert_kernel,
        grid_spec=grid_spec,
        out_shape=jax.ShapeDtypeStruct(xs.shape, BF16),
        compiler_params=_params("arbitrary", "arbitrary"),
        name="moe_experts",
    )(te, trow, nval, xs, w_gu, w_gu, w_down)


def _combine_kernel(tbl_ref, ys_ref, meta_ref, x_ref, g_ref, o_ref, stage_ref, sem):
    tile = pl.program_id(0)
    buf = tile % 2

    @pl.when(tile == 0)
    def _():
        stage_ref[...] = jnp.zeros_like(stage_ref)
        _start_runs(tbl_ref, tile, stage_ref.at[buf], ys_ref, sem.at[buf], False)

    @pl.when(tile + 1 < pl.num_programs(0))
    def _():
        _start_runs(tbl_ref, tile + 1, stage_ref.at[1 - buf], ys_ref, sem.at[1 - buf], False)

    _wait_runs(tbl_ref, tile, stage_ref.at[buf], ys_ref, sem.at[buf], False)
    stage = stage_ref[buf]
    half = ROUTE_TILE // 2
    rows = lax.broadcasted_iota(I32, (half, ROUTE_ROWS), 1)
    picked = []
    for p in (slice(0, half), slice(half, ROUTE_TILE)):
        meta = meta_ref[p, :]
        y1 = jnp.dot((rows == meta[:, 0:1].astype(I32)).astype(BF16), stage, preferred_element_type=F32)
        y2 = jnp.dot((rows == meta[:, 1:2].astype(I32)).astype(BF16), stage, preferred_element_type=F32)
        picked.append((p, meta, y1, y2))
    for p, meta, y1, y2 in picked:
        x = x_ref[p, :] + (meta[:, 2:3] * y1 + meta[:, 3:4] * y2)
        o_ref[p, :] = _rms(x, g_ref[...])


def _combine_call(tbl, ys, meta, x, gain):
    n = x.shape[0]
    row = pl.BlockSpec((ROUTE_TILE, D_MODEL), lambda i, tbl: (i, 0))
    grid_spec = pltpu.PrefetchScalarGridSpec(
        num_scalar_prefetch=1,
        grid=(n // ROUTE_TILE,),
        in_specs=[pl.BlockSpec(memory_space=pl.ANY),
                  pl.BlockSpec((ROUTE_TILE, LANES), lambda i, tbl: (i, 0)),
                  row,
                  pl.BlockSpec((1, D_MODEL), lambda i, tbl: (0, 0))],
        out_specs=row,
        scratch_shapes=[pltpu.VMEM((2, ROUTE_ROWS, D_MODEL), BF16), pltpu.SemaphoreType.DMA((2,))],
    )
    return pl.pallas_call(
        _combine_kernel,
        grid_spec=grid_spec,
        out_shape=jax.ShapeDtypeStruct((n, D_MODEL), F32),
        compiler_params=_params("arbitrary"),
        name="moe_combine",
    )(tbl, ys, meta, x, gain)


def _routing_tables(counts, n_row_tiles, tile_rows):
    padded = (counts + (ROUTE_ALIGN - 1)) // ROUTE_ALIGN * ROUTE_ALIGN
    stage_off = jnp.cumsum(padded, axis=1) - padded
    total = jnp.sum(padded, axis=0)
    tiles_e = (total + (tile_rows - 1)) // tile_rows
    region = jnp.cumsum(tiles_e) - tiles_e
    dst = region[None, :] * tile_rows + jnp.cumsum(padded, axis=0) - padded
    tbl = jnp.concatenate([stage_off, padded, dst], axis=1).reshape(-1).astype(I32)

    used = jnp.sum(tiles_e)
    ids = jnp.arange(n_row_tiles, dtype=I32)
    ends = jnp.cumsum(tiles_e)
    te = jnp.sum((ids[:, None] >= ends[None, :]).astype(I32), axis=1)
    te = jnp.minimum(te, N_EXPERTS - 1)
    nval = jnp.clip(total[te] - (ids - region[te]) * tile_rows, 0, tile_rows)
    valid = ids < used
    nval = jnp.where(valid, nval, 0).astype(I32)
    last = jnp.maximum(used - 1, 0)
    te = jnp.where(valid, te, te[last]).astype(I32)
    trow = jnp.where(valid, ids, n_row_tiles).astype(I32)
    return tbl, te, trow, nval


def _moe_final(y, w_out, x, gain, w_router_stack, w_gu, w_down, final_gain, fc):
    n = x.shape[0]
    n_tiles = n // ROUTE_TILE
    tile_rows = min(EXPERT_TILE, -(-2 * n // N_EXPERTS // MXU_COLS) * MXU_COLS)
    max_rows = 2 * n + n_tiles * N_EXPERTS * (ROUTE_ALIGN - 1)
    n_row_tiles = -(-max_rows // tile_rows) + N_EXPERTS
    x, h, meta, cnt = _router_call(y, w_out, x, gain, w_router_stack)
    tbl, te, trow, nval = _routing_tables(cnt[:, 0, :N_EXPERTS], n_row_tiles, tile_rows)
    xs = _dispatch_call(tbl, h, meta, (n_row_tiles + 1) * tile_rows)
    ys = _expert_call(te, trow, nval, xs, w_gu, w_down, fc, tile_rows)
    return _combine_call(tbl, ys, meta, x, final_gain)


def _stream(x, cache_k, cache_v, s0_t, lb1, weights, tm, chunk, sub):
    (norm_mix, norm_ch, norm_final, w_qkv, w_o, rel_table, w_in, w_out, out_norm,
     w_gu, w_down, w_router, w_egu, w_edown) = weights
    batch, seq, _ = x.shape
    n = batch * seq
    x = x.reshape(n, D_MODEL)
    row = lambda a: a.reshape(1, D_MODEL)

    if cache_k is None:
        q, k, v, kf, vf = _qkv_call(x, row(norm_mix[0]), w_qkv, seq // tm, tm)
        att = _band_attn_call(q, k, v, _band_bias(rel_table), batch, seq)
        keep = min(WINDOW, seq)
        new_k = kf.reshape(batch, keep, ATT_HEADS, ATT_HEAD_DIM)
        new_v = vf.reshape(batch, keep, ATT_HEADS, ATT_HEAD_DIM)
    else:
        q, k, v, kf, vf = _qkv_call(x, row(norm_mix[0]), w_qkv, 1, tm)
        w = cache_k.shape[1]
        att = _cache_attn_call(q, k, v, cache_k.reshape(batch, w, D_MODEL), cache_v.reshape(batch, w, D_MODEL),
                               _cache_bias(rel_table, w, seq), batch, seq)
        new_k = kf.reshape(batch, seq, ATT_HEADS, ATT_HEAD_DIM)
        new_v = vf.reshape(batch, seq, ATT_HEADS, ATT_HEAD_DIM)
    x = _ffn_call(att, w_o, x, row(norm_ch[0]), w_gu, w_down, min(n, 2 * tm), w_down.shape[0] // 2)

    q, f, i, z = _hg_in_call(x, row(norm_mix[1]), w_in, tm)
    y, s_t = _hg_scan_call(q, f, i, z, lb1, row(out_norm), s0_t, batch, seq, chunk, sub)
    y = _moe_final(y, w_out, x, row(norm_ch[1]), w_router, w_egu, w_edown, row(norm_final),
                   w_edown.shape[1] // 2)
    return y.reshape(batch, seq, D_MODEL), new_k, new_v, jnp.swapaxes(s_t, -1, -2)


def kernel(x_prompt, x_sample, cache_k, cache_v, state_hgrn, norm_mix, norm_ch, norm_final, att_w_qkv, att_w_o, att_rel_bias, hg_w_in, hg_w_out, hg_out_norm, hg_lower_bounds, ffn_w_gu, ffn_w_down, moe_w_router, moe_w_gu, moe_w_down):
    lb = jnp.cumsum(jax.nn.softmax(hg_lower_bounds.astype(F32), axis=0), axis=0)
    lb1 = (lb[1] - lb[0]).reshape(1, D_MODEL)
    w_router = jnp.pad(moe_w_router[0].astype(F32), ((0, 0), (0, LANES - N_EXPERTS)))
    w_router_hi = w_router.astype(BF16)
    w_router_lo = (w_router - w_router_hi.astype(F32)).astype(BF16)
    w_router = jnp.concatenate([w_router_hi, w_router_hi, w_router_lo], axis=0)
    weights = (norm_mix.astype(F32), norm_ch.astype(F32), norm_final.astype(F32),
               att_w_qkv[0].astype(BF16), att_w_o[0].astype(BF16), att_rel_bias[0],
               hg_w_in[0].astype(BF16), hg_w_out[0].astype(BF16), hg_out_norm[0].astype(F32),
               ffn_w_gu[0].astype(BF16), ffn_w_down[0].astype(BF16),
               w_router, moe_w_gu[0].astype(BF16), moe_w_down[0].astype(BF16))

    yp, kp, vp, sp = _stream(x_prompt, None, None, None, lb1, weights, 512, 128, 32)
    s0_t = jnp.swapaxes(state_hgrn[0].astype(F32), -1, -2)
    dec_tokens = x_sample.shape[0] * x_sample.shape[1]
    ys, ks, vs, ss = _stream(x_sample, cache_k[0], cache_v[0], s0_t, lb1, weights,
                             dec_tokens, x_sample.shape[1], x_sample.shape[1])
    return (yp, ys, kp[None], vp[None], ks[None], vs[None], sp[None], ss[None])
```

```python
import functools

import jax
import jax.numpy as jnp
import numpy as np
from jax import lax
from jax.experimental import pallas as pl
from jax.experimental.pallas import tpu as pltpu

F32 = jnp.float32
BF16 = jnp.bfloat16
I32 = jnp.int32

D_MODEL = 1024
ATT_HEADS = 16
ATT_HEAD_DIM = 64
CHUNK = 64
LEFT_CHUNKS = 8
WINDOW = LEFT_CHUNKS * CHUNK
REL_CLIP = 128
HG_HEADS = 8
HG_EXPAND = 128
N_EXPERTS = 8
RMS_EPS = 1e-6

LANES = 128
BF16_SUBLANES = 16
MXU_COLS = 256
VMEM_LIMIT = 56 * 1024 * 1024

HEAD_PAIRS = ATT_HEADS // 2
Q_GROUP = 2 * WINDOW
Q_HALF = WINDOW // 2
K_SPAN = Q_HALF + WINDOW
PAIRS_PER_STEP = 4
MASKED = -1e30

ROUTE_TILE = 512
ROUTE_ALIGN = BF16_SUBLANES
ROUTE_ROWS = 2 * ROUTE_TILE + N_EXPERTS * (ROUTE_ALIGN - 1)
ROUTE_ROWS = -(-ROUTE_ROWS // LANES) * LANES
RUN_SIZES = tuple(ROUTE_TILE >> s for s in range(6))
EXPERT_TILE = 1024

SCAN_CHUNKS_PER_STEP = 4
ROUTER_PARTS = 4

_NT = (((1,), (1,)), ((), ()))
_TN = (((0,), (0,)), ((), ()))


def _params(*sem):
    return pltpu.CompilerParams(dimension_semantics=sem, vmem_limit_bytes=VMEM_LIMIT)


def _rms(x, gain):
    return x * lax.rsqrt(jnp.mean(x * x, axis=-1, keepdims=True) + RMS_EPS) * gain


def _sigmoid(x):
    return 1.0 / (1.0 + jnp.exp(-x))


def _const_spec(shape):
    zeros = (0,) * len(shape)
    return pl.BlockSpec(shape, lambda *_: zeros)


def _qkv_kernel(x_ref, g_ref, w_ref, q_ref, k_ref, v_ref, kf_ref, vf_ref, *, tiles_per_seq):
    h = _rms(x_ref[...], g_ref[...]).astype(BF16)
    d = D_MODEL
    q = jnp.dot(h, w_ref[:, 0:d], preferred_element_type=F32) * (ATT_HEAD_DIM ** -0.5)
    k = jnp.dot(h, w_ref[:, d:2 * d], preferred_element_type=F32)
    v = jnp.dot(h, w_ref[:, 2 * d:3 * d], preferred_element_type=F32)
    for p in range(HEAD_PAIRS):
        sl = slice(p * LANES, (p + 1) * LANES)
        q_ref[p] = q[:, sl].astype(BF16)
        k_ref[p] = k[:, sl].astype(BF16)
        v_ref[p] = v[:, sl].astype(BF16)

    @pl.when(pl.program_id(0) % tiles_per_seq == tiles_per_seq - 1)
    def _():
        kf_ref[...] = k
        vf_ref[...] = v


def _qkv_call(x, gain, w, tiles_per_seq, tm):
    n = x.shape[0]
    n_tiles = n // tm
    n_seq = n_tiles // tiles_per_seq
    slab = jax.ShapeDtypeStruct((HEAD_PAIRS, n, LANES), BF16)
    tail = jax.ShapeDtypeStruct((n_seq * tm, D_MODEL), F32)
    slab_spec = pl.BlockSpec((HEAD_PAIRS, tm, LANES), lambda i: (0, i, 0))
    tail_spec = pl.BlockSpec((tm, D_MODEL), lambda i: (i // tiles_per_seq, 0))
    return pl.pallas_call(
        functools.partial(_qkv_kernel, tiles_per_seq=tiles_per_seq),
        grid=(n_tiles,),
        in_specs=[pl.BlockSpec((tm, D_MODEL), lambda i: (i, 0)),
                  _const_spec((1, D_MODEL)),
                  _const_spec((D_MODEL, 3 * D_MODEL))],
        out_specs=[slab_spec, slab_spec, slab_spec, tail_spec, tail_spec],
        out_shape=[slab, slab, slab, tail, tail],
        compiler_params=_params("arbitrary"),
        name="qkv_proj",
    )(x, gain, w)


def _band_attn_kernel(q_ref, kp_ref, kc_ref, vp_ref, vc_ref, bias_ref, start_ref, o_ref):
    first_group = pl.program_id(2) == 0
    lane = lax.broadcasted_iota(I32, (2 * Q_HALF, LANES), 1)
    row = lax.broadcasted_iota(I32, (2 * Q_HALF, LANES), 0)
    own = (lane < ATT_HEAD_DIM) == (row < Q_HALF)
    first_head = lax.broadcasted_iota(I32, (Q_HALF, LANES), 1) < ATT_HEAD_DIM
    def window(prev_ref, cur_ref, pr, lo):
        hi = lo + K_SPAN
        if lo >= WINDOW:
            return cur_ref[pr, lo - WINDOW:hi - WINDOW, :]
        return jnp.concatenate([prev_ref[pr, lo:, :], cur_ref[pr, 0:hi - WINDOW, :]], axis=0)

    for pr in range(PAIRS_PER_STEP):
        for part in range(q_ref.shape[1] // Q_HALF):
            rows = slice(part * Q_HALF, (part + 1) * Q_HALF)
            qh = q_ref[pr, rows, :]
            q2 = jnp.concatenate([qh, qh], axis=0)
            q2 = jnp.where(own, q2, jnp.zeros_like(q2))
            keys = window(kp_ref, kc_ref, pr, part * Q_HALF)
            vals = window(vp_ref, vc_ref, pr, part * Q_HALF)
            s = lax.dot_general(q2, keys, _NT, preferred_element_type=F32) + bias_ref[pr]
            if (part + 1) * Q_HALF <= WINDOW:
                s = s + jnp.where(first_group, start_ref[part:part + 1, :], 0.0)
            m = jnp.max(s, axis=-1, keepdims=True)
            p = jnp.exp(s - m)
            l = jnp.sum(p, axis=-1, keepdims=True)
            o2 = jnp.dot(p.astype(BF16), vals, preferred_element_type=F32) / l
            o = jnp.where(first_head, o2[:Q_HALF], o2[Q_HALF:])
            o_ref[rows, pr * LANES:(pr + 1) * LANES] = o.astype(BF16)


def _band_bias(table):
    qi = np.arange(Q_HALF)[:, None]
    kj = np.arange(K_SPAN)[None, :]
    cq, ck = qi // CHUNK, kj // CHUNK
    band = (ck >= cq) & (ck <= cq + LEFT_CHUNKS)
    start = np.where(np.concatenate([kj >= WINDOW, kj >= WINDOW - Q_HALF]), 0.0, MASKED).astype(np.float32)
    table = table.astype(F32)
    n_rel = table.shape[1]
    span = Q_HALF + K_SPAN - 1
    lo_rep = (K_SPAN - 1 - WINDOW) - (CHUNK - 1)
    hi_rep = span - lo_rep - n_rel
    line = jnp.concatenate([jnp.repeat(table[:, :1], lo_rep, axis=1), table,
                            jnp.repeat(table[:, -1:], hi_rep, axis=1)], axis=1)[:, ::-1]
    flow = jnp.tile(line, (1, Q_HALF + 1))[:, Q_HALF - 1:Q_HALF - 1 + Q_HALF * (span - 1)]
    b = flow.reshape(-1, Q_HALF, span - 1)[:, :, :K_SPAN]
    b = jnp.where(band[None], b, MASKED)
    return b.reshape(HEAD_PAIRS, 2 * Q_HALF, K_SPAN), jnp.asarray(start)


def _band_attn_call(q, k, v, bias, start, batch, seq):
    groups = seq // Q_GROUP
    n = batch * seq

    def cur(p, b, g):
        return (p, b * groups + g, 0)

    def prev(p, b, g):
        return (p, (b * groups + g) * (Q_GROUP // WINDOW) - jnp.minimum(g, 1), 0)

    blk = (PAIRS_PER_STEP, Q_GROUP, LANES)
    blk_prev = (PAIRS_PER_STEP, WINDOW, LANES)
    return pl.pallas_call(
        _band_attn_kernel,
        grid=(HEAD_PAIRS // PAIRS_PER_STEP, batch, groups),
        in_specs=[pl.BlockSpec(blk, cur), pl.BlockSpec(blk_prev, prev), pl.BlockSpec(blk, cur),
                  pl.BlockSpec(blk_prev, prev), pl.BlockSpec(blk, cur),
                  pl.BlockSpec((PAIRS_PER_STEP, 2 * Q_HALF, K_SPAN), lambda p, b, g: (p, 0, 0)),
                  _const_spec((2, K_SPAN))],
        out_specs=pl.BlockSpec((Q_GROUP, PAIRS_PER_STEP * LANES), lambda p, b, g: (b * groups + g, p)),
        out_shape=jax.ShapeDtypeStruct((n, D_MODEL), BF16),
        compiler_params=_params("arbitrary", "arbitrary", "arbitrary"),
        name="band_attention",
    )(q, k, k, v, v, bias, start)


def _cache_attn_kernel(q_ref, k_ref, v_ref, ck_ref, cv_ref, bias_ref, o_ref):
    s_len = q_ref.shape[1]
    lane = lax.broadcasted_iota(I32, (s_len, LANES), 1)
    low = lane < ATT_HEAD_DIM
    scores = []
    for p in range(HEAD_PAIRS):
        sl = slice(p * LANES, (p + 1) * LANES)
        keys = jnp.concatenate([ck_ref[:, sl].astype(BF16), k_ref[p]], axis=0)
        qp = q_ref[p]
        for hd in range(2):
            qm = jnp.where(low if hd == 0 else jnp.logical_not(low), qp, jnp.zeros_like(qp))
            scores.append(lax.dot_general(qm, keys, _NT, preferred_element_type=F32) + bias_ref[2 * p + hd])
    probs = []
    for s in scores:
        m = jnp.max(s, axis=-1, keepdims=True)
        e = jnp.exp(s - m)
        probs.append((e.astype(BF16), jnp.sum(e, axis=-1, keepdims=True)))
    for p in range(HEAD_PAIRS):
        sl = slice(p * LANES, (p + 1) * LANES)
        vals = jnp.concatenate([cv_ref[:, sl].astype(BF16), v_ref[p]], axis=0)
        outs = [jnp.dot(e, vals, preferred_element_type=F32) / l for e, l in probs[2 * p:2 * p + 2]]
        o_ref[:, sl] = jnp.where(low, outs[0], outs[1]).astype(BF16)


def _cache_bias(table, w, s_len):
    qi = np.arange(s_len)[:, None]
    kj = np.arange(w + s_len)[None, :]
    idx = np.clip(w + qi - kj, -(CHUNK - 1), REL_CLIP) + (CHUNK - 1)
    return table.astype(F32)[:, idx]


def _cache_attn_call(q, k, v, cache_k, cache_v, bias, batch, s_len):
    w = cache_k.shape[1]
    slab = pl.BlockSpec((HEAD_PAIRS, s_len, LANES), lambda b: (0, b, 0))
    cache = pl.BlockSpec((None, w, D_MODEL), lambda b: (b, 0, 0))
    return pl.pallas_call(
        _cache_attn_kernel,
        grid=(batch,),
        in_specs=[slab, slab, slab, cache, cache, _const_spec((ATT_HEADS, s_len, w + s_len))],
        out_specs=pl.BlockSpec((s_len, D_MODEL), lambda b: (b, 0)),
        out_shape=jax.ShapeDtypeStruct((batch * s_len, D_MODEL), BF16),
        compiler_params=_params("arbitrary"),
        name="cache_attention",
    )(q, k, v, cache_k, cache_v, bias)


def _swiglu_block(h, wg_ref, wu_ref, wd_ref, act_ref):
    fc = act_ref.shape[1]
    for c0 in range(0, fc, MXU_COLS):
        sl = slice(c0, min(c0 + MXU_COLS, fc))
        gate = jnp.dot(h, wg_ref[:, sl], preferred_element_type=F32)
        up = jnp.dot(h, wu_ref[:, sl], preferred_element_type=F32)
        act_ref[:, sl] = (gate * _sigmoid(gate) * up).astype(BF16)
    return jnp.dot(act_ref[...], wd_ref[...].astype(BF16), preferred_element_type=F32)


def _ffn_kernel(a_ref, wo_ref, x_ref, g_ref, wg_ref, wu_ref, wd_ref, o_ref, h_ref, act_ref):
    j = pl.program_id(1)

    @pl.when(j == 0)
    def _():
        x1 = x_ref[...] + jnp.dot(a_ref[...], wo_ref[...], preferred_element_type=F32)
        o_ref[...] = x1
        h_ref[...] = _rms(x1, g_ref[...]).astype(BF16)

    o_ref[...] += _swiglu_block(h_ref[...], wg_ref, wu_ref, wd_ref, act_ref)


def _ffn_call(a, w_o, x, gain, w_gu, w_down, tm, fc):
    n = x.shape[0]
    d_ff = w_down.shape[0]
    nf = d_ff // fc
    row = pl.BlockSpec((tm, D_MODEL), lambda i, j: (i, 0))
    return pl.pallas_call(
        _ffn_kernel,
        grid=(n // tm, nf),
        in_specs=[row, _const_spec((D_MODEL, D_MODEL)), row, _const_spec((1, D_MODEL)),
                  pl.BlockSpec((D_MODEL, fc), lambda i, j: (0, j)),
                  pl.BlockSpec((D_MODEL, fc), lambda i, j: (0, j + nf)),
                  pl.BlockSpec((fc, D_MODEL), lambda i, j: (j, 0))],
        out_specs=row,
        out_shape=jax.ShapeDtypeStruct((n, D_MODEL), F32),
        scratch_shapes=[pltpu.VMEM((tm, D_MODEL), BF16), pltpu.VMEM((tm, fc), BF16)],
        compiler_params=_params("arbitrary", "arbitrary"),
        name="swiglu_ffn",
    )(a, w_o, x, gain, w_gu, w_gu, w_down)


def _hg_in_kernel(x_ref, g_ref, w_ref, lb_ref, q_ref, f_ref, v_ref, z_ref):
    d = D_MODEL
    lb = lb_ref[...]
    half = x_ref.shape[0] // 2
    for p in (slice(0, half), slice(half, 2 * half)):
        h = _rms(x_ref[p, :], g_ref[...]).astype(BF16)

        def proj(idx):
            return jnp.dot(h, w_ref[:, idx * d:(idx + 1) * d], preferred_element_type=F32)

        f_ref[p, :] = lb + (1.0 - lb) * _sigmoid(proj(1))
        z = proj(3)
        z_ref[p, :] = z * _sigmoid(z)
        q_ref[p, :] = proj(0) * (HG_EXPAND ** -0.5)
        v_ref[p, :] = proj(2).astype(BF16)


def _hg_in_call(x, gain, w, lb, tm):
    n = x.shape[0]
    row = pl.BlockSpec((tm, D_MODEL), lambda i: (i, 0))
    out = jax.ShapeDtypeStruct((n, D_MODEL), F32)
    return pl.pallas_call(
        _hg_in_kernel,
        grid=(n // tm,),
        in_specs=[row, _const_spec((1, D_MODEL)), _const_spec((D_MODEL, 4 * D_MODEL)),
                  _const_spec((1, D_MODEL))],
        out_specs=[row, row, row, row],
        out_shape=[out, out, jax.ShapeDtypeStruct((n, D_MODEL), BF16), out],
        compiler_params=_params("arbitrary"),
        name="hgrn_in_proj",
    )(x, gain, w, lb)


def _hg_scan_kernel(*refs, chunk, sub, has_s0):
    st_refs = refs[-HG_HEADS:]
    refs = refs[:-HG_HEADS]
    if has_s0:
        q_ref, f_ref, v_ref, z_ref, on_ref, s0_ref, y_ref, sf_ref = refs
    else:
        q_ref, f_ref, v_ref, z_ref, on_ref, y_ref, sf_ref = refs
    t = pl.program_id(1)

    @pl.when(t == 0)
    def _():
        for hd in range(HG_HEADS):
            st_refs[hd][...] = s0_ref[hd] if has_s0 else jnp.zeros((HG_EXPAND, HG_EXPAND), F32)

    r = lax.broadcasted_iota(I32, (chunk, chunk), 0)
    c = lax.broadcasted_iota(I32, (chunk, chunk), 1)
    causal = c <= r
    tri = causal.astype(BF16)
    n_sub = chunk // sub
    for c0 in range(0, q_ref.shape[0], chunk):
        _hg_scan_chunk(slice(c0, c0 + chunk), q_ref, f_ref, v_ref, z_ref, on_ref, y_ref, st_refs,
                       r, causal, tri, chunk, sub, n_sub)

    @pl.when(t == pl.num_programs(1) - 1)
    def _():
        for hd in range(HG_HEADS):
            sf_ref[hd] = st_refs[hd][...]


def _hg_scan_chunk(rows, q_ref, f_ref, v_ref, z_ref, on_ref, y_ref, st_refs,
                   r, causal, tri, chunk, sub, n_sub):
    forget = f_ref[rows, :]
    log_f = jnp.log(forget)
    key_all = 1.0 - forget
    hi = log_f.astype(BF16)
    rest = log_f - hi.astype(F32)
    mid = rest.astype(BF16)
    lo = (rest - mid.astype(F32)).astype(BF16)
    g_all = jnp.dot(jnp.concatenate([tri, tri, tri], axis=1), jnp.concatenate([hi, mid, lo], axis=0),
                    preferred_element_type=F32)

    heads = [slice(hd * HG_EXPAND, (hd + 1) * HG_EXPAND) for hd in range(HG_HEADS)]
    values, inter, scores = [], [], []
    for hd, sl in enumerate(heads):
        g = g_all[:, sl]
        q = q_ref[rows, sl]
        k = key_all[:, sl]
        v = v_ref[rows, sl]
        g_last = g[chunk - 1:chunk, :]
        state = st_refs[hd][...]
        q_in = (q * jnp.exp(g)).astype(BF16)
        k_end = (k * jnp.exp(g_last - g)).astype(BF16)
        o_inter = lax.dot_general(q_in, state.astype(BF16), _NT, preferred_element_type=F32)
        st_refs[hd][...] = state * jnp.exp(g_last) + lax.dot_general(v, k_end, _TN, preferred_element_type=F32)

        g_mids = [g[b * sub + sub // 2 - 1:b * sub + sub // 2, :] for b in range(n_sub)]
        g_mid_rows = jnp.concatenate([jnp.broadcast_to(m, (sub, HG_EXPAND)) for m in g_mids], axis=0)
        q_sub = (q * jnp.exp(g - g_mid_rows)).astype(BF16)
        k_blocks = []
        scaled = None
        for b in range(n_sub):
            blk = slice(b * sub, (b + 1) * sub)
            fresh = k[blk] * jnp.exp(g_mids[b] - g[blk])
            if b == 0:
                scaled = fresh
            else:
                scaled = jnp.concatenate([scaled * jnp.exp(g_mids[b] - g_mids[b - 1]), fresh], axis=0)
            k_blocks.append(scaled.astype(BF16))
            if (b + 1) * sub < chunk:
                k_blocks.append(jnp.zeros((chunk - (b + 1) * sub, HG_EXPAND), BF16))
        k_sub = k_blocks[0] if len(k_blocks) == 1 else jnp.concatenate(k_blocks, axis=0)
        values.append(v)
        inter.append(o_inter)
        scores.append(lax.dot_general(q_sub, k_sub, _NT, preferred_element_type=F32))

    outs = []
    for v, o_inter, a_all in zip(values, inter, scores):
        a = a_all[:, 0:chunk]
        for b in range(1, n_sub):
            a = jnp.where(r >= b * sub, a_all[:, b * chunk:(b + 1) * chunk], a)
        a = jnp.where(causal, a, 0.0).astype(BF16)
        outs.append(o_inter + jnp.dot(a, v, preferred_element_type=F32))
    for sl, o in zip(heads, outs):
        o = o * lax.rsqrt(jnp.mean(o * o, axis=-1, keepdims=True) + RMS_EPS) * on_ref[:, sl]
        y_ref[rows, sl] = (o * z_ref[rows, sl]).astype(BF16)


def _hg_scan_call(q, f, v, z, out_norm, s0_t, batch, seq, chunk, sub):
    step_rows = min(seq, SCAN_CHUNKS_PER_STEP * chunk)
    n_chunks = seq // step_rows
    has_s0 = s0_t is not None
    row = pl.BlockSpec((step_rows, D_MODEL), lambda b, t: (b * n_chunks + t, 0))
    st = pl.BlockSpec((None, HG_HEADS, HG_EXPAND, HG_EXPAND), lambda b, t: (b, 0, 0, 0))
    in_specs = [row, row, row, row, _const_spec((1, D_MODEL))]
    args = [q, f, v, z, out_norm]
    if has_s0:
        in_specs.append(st)
        args.append(s0_t)
    return pl.pallas_call(
        functools.partial(_hg_scan_kernel, chunk=chunk, sub=sub, has_s0=has_s0),
        grid=(batch, n_chunks),
        in_specs=in_specs,
        out_specs=[row, st],
        out_shape=[jax.ShapeDtypeStruct((batch * seq, D_MODEL), BF16),
                   jax.ShapeDtypeStruct((batch, HG_HEADS, HG_EXPAND, HG_EXPAND), F32)],
        scratch_shapes=[pltpu.VMEM((HG_EXPAND, HG_EXPAND), F32)] * HG_HEADS,
        compiler_params=_params("arbitrary", "arbitrary"),
        name="hgrn_scan",
    )(*args)


def _router_kernel(y_ref, wo_ref, x_ref, g_ref, wr_ref, xo_ref, h_ref, meta_ref, cnt_ref):
    tr = x_ref.shape[0]
    parts = [slice(p * (tr // ROUTER_PARTS), (p + 1) * (tr // ROUTER_PARTS)) for p in range(ROUTER_PARTS)]
    xs = [x_ref[p, :] + jnp.dot(y_ref[p, :], wo_ref[...], preferred_element_type=F32) for p in parts]
    for p, x in zip(parts, xs):
        xo_ref[p, :] = x
    hfs = [_rms(x, g_ref[...]) for x in xs]
    logit_parts = []
    for p, hf in zip(parts, hfs):
        hi = hf.astype(BF16)
        lo = (hf - hi.astype(F32)).astype(BF16)
        h_ref[p, :] = hi
        logit_parts.append(jnp.dot(jnp.concatenate([hi, lo, hi], axis=1), wr_ref[...],
                                   preferred_element_type=F32))
    logits = jnp.concatenate(logit_parts, axis=0)
    lane = lax.broadcasted_iota(I32, (tr, LANES), 1).astype(F32)
    lg = jnp.where(lane < N_EXPERTS, logits, -jnp.inf)
    m1 = jnp.max(lg, axis=-1, keepdims=True)
    i1 = jnp.min(jnp.where(lg == m1, lane, float(LANES)), axis=-1, keepdims=True)
    lg2 = jnp.where(lane == i1, -jnp.inf, lg)
    m2 = jnp.max(lg2, axis=-1, keepdims=True)
    i2 = jnp.min(jnp.where(lg2 == m2, lane, float(LANES)), axis=-1, keepdims=True)
    e = jnp.exp(m2 - m1)
    gate1 = 1.0 / (1.0 + e)
    gate2 = e / (1.0 + e)

    chosen = jnp.logical_or(lane == i1, lane == i2)
    r = lax.broadcasted_iota(I32, (tr, tr), 0)
    c = lax.broadcasted_iota(I32, (tr, tr), 1)
    before = (c < r).astype(BF16)
    rank = jnp.dot(before, chosen.astype(BF16), preferred_element_type=F32)
    counts = jnp.sum(chosen.astype(F32), axis=0, keepdims=True)
    padded = jnp.floor((counts + (ROUTE_ALIGN - 1)) * (1.0 / ROUTE_ALIGN)) * ROUTE_ALIGN
    r2 = lax.broadcasted_iota(I32, (LANES, LANES), 0)
    c2 = lax.broadcasted_iota(I32, (LANES, LANES), 1)
    run_start = jnp.dot(jnp.broadcast_to(padded, (8, LANES)).astype(BF16), (r2 < c2).astype(BF16),
                        preferred_element_type=F32)[0:1]
    slot = run_start + rank
    slot1 = jnp.sum(jnp.where(lane == i1, slot, 0.0), axis=-1, keepdims=True)
    slot2 = jnp.sum(jnp.where(lane == i2, slot, 0.0), axis=-1, keepdims=True)
    meta = jnp.where(lane == 0, slot1, 0.0)
    for idx, val in ((1, slot2), (2, gate1), (3, gate2), (4, i1), (5, i2)):
        meta = jnp.where(lane == idx, val, meta)
    meta_ref[...] = meta
    cnt_ref[...] = jnp.broadcast_to(counts.astype(I32)[None], cnt_ref.shape)


def _router_call(y, w_out, x, gain, w_router_stack):
    n = x.shape[0]
    n_tiles = n // ROUTE_TILE
    row = pl.BlockSpec((ROUTE_TILE, D_MODEL), lambda i: (i, 0))
    return pl.pallas_call(
        _router_kernel,
        grid=(n_tiles,),
        in_specs=[row, _const_spec((D_MODEL, D_MODEL)), row, _const_spec((1, D_MODEL)),
                  _const_spec((3 * D_MODEL, LANES))],
        out_specs=[row, row, pl.BlockSpec((ROUTE_TILE, LANES), lambda i: (i, 0)),
                   pl.BlockSpec((1, 8, LANES), lambda i: (i, 0, 0))],
        out_shape=[jax.ShapeDtypeStruct((n, D_MODEL), F32),
                   jax.ShapeDtypeStruct((n, D_MODEL), BF16),
                   jax.ShapeDtypeStruct((n, LANES), F32),
                   jax.ShapeDtypeStruct((n_tiles, 8, LANES), I32)],
        compiler_params=_params("arbitrary"),
        name="moe_router",
    )(y, w_out, x, gain, w_router_stack)


def _run_copies(tbl_ref, tile, stage_ref, hbm_ref, sem, to_hbm):
    base = tile * (3 * N_EXPERTS)
    for e in range(N_EXPERTS):
        lo = tbl_ref[base + e]
        n = tbl_ref[base + N_EXPERTS + e]
        dst = tbl_ref[base + 2 * N_EXPERTS + e]
        for sz in RUN_SIZES:
            done = jnp.bitwise_and(n, ~(2 * sz - 1))
            st = stage_ref.at[pl.ds(pl.multiple_of(lo + done, ROUTE_ALIGN), sz)]
            hb = hbm_ref.at[pl.ds(pl.multiple_of(dst + done, ROUTE_ALIGN), sz)]
            copy = pltpu.make_async_copy(st, hb, sem) if to_hbm else pltpu.make_async_copy(hb, st, sem)
            yield jnp.bitwise_and(n, sz) != 0, copy


def _start_runs(tbl_ref, tile, stage_ref, hbm_ref, sem, to_hbm):
    for cond, copy in _run_copies(tbl_ref, tile, stage_ref, hbm_ref, sem, to_hbm):
        pl.when(cond)(copy.start)


def _wait_runs(tbl_ref, tile, stage_ref, hbm_ref, sem, to_hbm):
    for cond, copy in _run_copies(tbl_ref, tile, stage_ref, hbm_ref, sem, to_hbm):
        pl.when(cond)(copy.wait)


def _dispatch_kernel(tbl_ref, h_ref, meta_ref, xs_ref, stage_ref, sem):
    tile = pl.program_id(0)
    buf = tile % 2
    meta = meta_ref[...]
    slot1 = meta[:, 0:1].astype(I32)
    slot2 = meta[:, 1:2].astype(I32)
    rows = lax.broadcasted_iota(I32, (ROUTE_TILE, ROUTE_ROWS), 1)
    pick = jnp.logical_or(rows == slot1, rows == slot2).astype(BF16)
    stage_ref[buf] = lax.dot_general(pick, h_ref[...], _TN, preferred_element_type=F32).astype(BF16)
    _start_runs(tbl_ref, tile, stage_ref.at[buf], xs_ref, sem.at[buf], True)

    @pl.when(tile > 0)
    def _():
        _wait_runs(tbl_ref, tile - 1, stage_ref.at[1 - buf], xs_ref, sem.at[1 - buf], True)

    @pl.when(tile == pl.num_programs(0) - 1)
    def _():
        _wait_runs(tbl_ref, tile, stage_ref.at[buf], xs_ref, sem.at[buf], True)


def _dispatch_call(tbl, h, meta, sorted_rows):
    n = h.shape[0]
    grid_spec = pltpu.PrefetchScalarGridSpec(
        num_scalar_prefetch=1,
        grid=(n // ROUTE_TILE,),
        in_specs=[pl.BlockSpec((ROUTE_TILE, D_MODEL), lambda i, tbl: (i, 0)),
                  pl.BlockSpec((ROUTE_TILE, LANES), lambda i, tbl: (i, 0))],
        out_specs=pl.BlockSpec(memory_space=pl.ANY),
        scratch_shapes=[pltpu.VMEM((2, ROUTE_ROWS, D_MODEL), BF16), pltpu.SemaphoreType.DMA((2,))],
    )
    return pl.pallas_call(
        _dispatch_kernel,
        grid_spec=grid_spec,
        out_shape=jax.ShapeDtypeStruct((sorted_rows, D_MODEL), BF16),
        compiler_params=_params("arbitrary"),
        name="moe_dispatch",
    )(tbl, h, meta)


def _expert_kernel(te_ref, trow_ref, nval_ref, x_ref, wg_ref, wu_ref, wd_ref, o_ref, act_ref, acc_ref):
    i = pl.program_id(0)
    j = pl.program_id(1)
    nv = nval_ref[i]

    @pl.when(nv > 0)
    def _():
        @pl.when(j == 0)
        def _():
            acc_ref[...] = jnp.zeros_like(acc_ref)

        x = x_ref[...]
        row = lax.broadcasted_iota(I32, x.shape, 0)
        x = jnp.where(row < nv, x, jnp.zeros_like(x))
        acc_ref[...] += _swiglu_block(x, wg_ref, wu_ref, wd_ref, act_ref)

        @pl.when(j == pl.num_programs(1) - 1)
        def _():
            o_ref[...] = acc_ref[...].astype(BF16)


def _expert_call(te, trow, nval, xs, w_gu, w_down, fc, tile_rows):
    n_tiles = te.shape[0]
    d_ff = w_down.shape[1]
    nf = d_ff // fc

    def jj(i, j, nval):
        return jnp.where(nval[i] > 0, j, nf - 1)

    grid_spec = pltpu.PrefetchScalarGridSpec(
        num_scalar_prefetch=3,
        grid=(n_tiles, nf),
        in_specs=[pl.BlockSpec((tile_rows, D_MODEL), lambda i, j, te, trow, nval: (trow[i], 0)),
                  pl.BlockSpec((None, D_MODEL, fc), lambda i, j, te, trow, nval: (te[i], 0, jj(i, j, nval))),
                  pl.BlockSpec((None, D_MODEL, fc),
                               lambda i, j, te, trow, nval: (te[i], 0, jj(i, j, nval) + nf)),
                  pl.BlockSpec((None, fc, D_MODEL), lambda i, j, te, trow, nval: (te[i], jj(i, j, nval), 0))],
        out_specs=pl.BlockSpec((tile_rows, D_MODEL), lambda i, j, te, trow, nval: (trow[i], 0)),
        scratch_shapes=[pltpu.VMEM((tile_rows, fc), BF16), pltpu.VMEM((tile_rows, D_MODEL), F32)],
    )
    return pl.pallas_call(
        _expert_kernel,
        grid_spec=grid_spec,
        out_shape=jax.ShapeDtypeStruct(xs.shape, BF16),
        compiler_params=_params("arbitrary", "arbitrary"),
        name="moe_experts",
    )(te, trow, nval, xs, w_gu, w_gu, w_down)


def _combine_kernel(tbl_ref, ys_ref, meta_ref, x_ref, g_ref, o_ref, stage_ref, sem):
    tile = pl.program_id(0)
    buf = tile % 2

    @pl.when(tile == 0)
    def _():
        stage_ref[...] = jnp.zeros_like(stage_ref)
        _start_runs(tbl_ref, tile, stage_ref.at[buf], ys_ref, sem.at[buf], False)

    @pl.when(tile + 1 < pl.num_programs(0))
    def _():
        _start_runs(tbl_ref, tile + 1, stage_ref.at[1 - buf], ys_ref, sem.at[1 - buf], False)

    _wait_runs(tbl_ref, tile, stage_ref.at[buf], ys_ref, sem.at[buf], False)
    stage = stage_ref[buf]
    half = ROUTE_TILE // 2
    rows = lax.broadcasted_iota(I32, (half, ROUTE_ROWS), 1)
    picked = []
    for p in (slice(0, half), slice(half, ROUTE_TILE)):
        meta = meta_ref[p, :]
        y1 = jnp.dot((rows == meta[:, 0:1].astype(I32)).astype(BF16), stage, preferred_element_type=F32)
        y2 = jnp.dot((rows == meta[:, 1:2].astype(I32)).astype(BF16), stage, preferred_element_type=F32)
        picked.append((p, meta, y1, y2))
    for p, meta, y1, y2 in picked:
        x = x_ref[p, :] + (meta[:, 2:3] * y1 + meta[:, 3:4] * y2)
        o_ref[p, :] = _rms(x, g_ref[...])


def _combine_call(tbl, ys, meta, x, gain):
    n = x.shape[0]
    row = pl.BlockSpec((ROUTE_TILE, D_MODEL), lambda i, tbl: (i, 0))
    grid_spec = pltpu.PrefetchScalarGridSpec(
        num_scalar_prefetch=1,
        grid=(n // ROUTE_TILE,),
        in_specs=[pl.BlockSpec(memory_space=pl.ANY),
                  pl.BlockSpec((ROUTE_TILE, LANES), lambda i, tbl: (i, 0)),
                  row,
                  pl.BlockSpec((1, D_MODEL), lambda i, tbl: (0, 0))],
        out_specs=row,
        scratch_shapes=[pltpu.VMEM((2, ROUTE_ROWS, D_MODEL), BF16), pltpu.SemaphoreType.DMA((2,))],
    )
    return pl.pallas_call(
        _combine_kernel,
        grid_spec=grid_spec,
        out_shape=jax.ShapeDtypeStruct((n, D_MODEL), F32),
        compiler_params=_params("arbitrary"),
        name="moe_combine",
    )(tbl, ys, meta, x, gain)


def _routing_tables(counts, n_row_tiles, tile_rows):
    padded = (counts + (ROUTE_ALIGN - 1)) // ROUTE_ALIGN * ROUTE_ALIGN
    stage_off = jnp.cumsum(padded, axis=1) - padded
    total = jnp.sum(padded, axis=0)
    tiles_e = (total + (tile_rows - 1)) // tile_rows
    region = jnp.cumsum(tiles_e) - tiles_e
    dst = region[None, :] * tile_rows + jnp.cumsum(padded, axis=0) - padded
    tbl = jnp.concatenate([stage_off, padded, dst], axis=1).reshape(-1).astype(I32)

    used = jnp.sum(tiles_e)
    ids = jnp.arange(n_row_tiles, dtype=I32)
    ends = jnp.cumsum(tiles_e)
    te = jnp.sum((ids[:, None] >= ends[None, :]).astype(I32), axis=1)
    te = jnp.minimum(te, N_EXPERTS - 1)
    nval = jnp.clip(total[te] - (ids - region[te]) * tile_rows, 0, tile_rows)
    valid = ids < used
    nval = jnp.where(valid, nval, 0).astype(I32)
    last = jnp.maximum(used - 1, 0)
    te = jnp.where(valid, te, te[last]).astype(I32)
    trow = jnp.where(valid, ids, n_row_tiles).astype(I32)
    return tbl, te, trow, nval


def _moe_final(y, w_out, x, gain, w_router_stack, w_gu, w_down, final_gain, fc):
    n = x.shape[0]
    n_tiles = n // ROUTE_TILE
    tile_rows = min(EXPERT_TILE, -(-2 * n // N_EXPERTS // MXU_COLS) * MXU_COLS)
    max_rows = 2 * n + n_tiles * N_EXPERTS * (ROUTE_ALIGN - 1)
    n_row_tiles = -(-max_rows // tile_rows) + N_EXPERTS
    x, h, meta, cnt = _router_call(y, w_out, x, gain, w_router_stack)
    tbl, te, trow, nval = _routing_tables(cnt[:, 0, :N_EXPERTS], n_row_tiles, tile_rows)
    xs = _dispatch_call(tbl, h, meta, (n_row_tiles + 1) * tile_rows)
    ys = _expert_call(te, trow, nval, xs, w_gu, w_down, fc, tile_rows)
    return _combine_call(tbl, ys, meta, x, final_gain)


def _stream(x, cache_k, cache_v, s0_t, lb1, weights, tm, chunk, sub):
    (norm_mix, norm_ch, norm_final, w_qkv, w_o, rel_table, w_in, w_out, out_norm,
     w_gu, w_down, w_router, w_egu, w_edown) = weights
    batch, seq, _ = x.shape
    n = batch * seq
    x = x.reshape(n, D_MODEL)
    row = lambda a: a.reshape(1, D_MODEL)

    if cache_k is None:
        q, k, v, kf, vf = _qkv_call(x, row(norm_mix[0]), w_qkv, seq // tm, tm)
        att = _band_attn_call(q, k, v, *_band_bias(rel_table), batch, seq)
        keep = min(WINDOW, seq)
        new_k = kf.reshape(batch, keep, ATT_HEADS, ATT_HEAD_DIM)
        new_v = vf.reshape(batch, keep, ATT_HEADS, ATT_HEAD_DIM)
    else:
        q, k, v, kf, vf = _qkv_call(x, row(norm_mix[0]), w_qkv, 1, tm)
        w = cache_k.shape[1]
        att = _cache_attn_call(q, k, v, cache_k.reshape(batch, w, D_MODEL), cache_v.reshape(batch, w, D_MODEL),
                               _cache_bias(rel_table, w, seq), batch, seq)
        new_k = kf.reshape(batch, seq, ATT_HEADS, ATT_HEAD_DIM)
        new_v = vf.reshape(batch, seq, ATT_HEADS, ATT_HEAD_DIM)
    x = _ffn_call(att, w_o, x, row(norm_ch[0]), w_gu, w_down, min(n, 2 * tm), w_down.shape[0] // 2)

    q, f, v, z = _hg_in_call(x, row(norm_mix[1]), w_in, lb1, tm)
    y, s_t = _hg_scan_call(q, f, v, z, row(out_norm), s0_t, batch, seq, chunk, sub)
    y = _moe_final(y, w_out, x, row(norm_ch[1]), w_router, w_egu, w_edown, row(norm_final),
                   w_edown.shape[1] // 2)
    return y.reshape(batch, seq, D_MODEL), new_k, new_v, jnp.swapaxes(s_t, -1, -2)


def kernel(x_prompt, x_sample, cache_k, cache_v, state_hgrn, norm_mix, norm_ch, norm_final, att_w_qkv, att_w_o, att_rel_bias, hg_w_in, hg_w_out, hg_out_norm, hg_lower_bounds, ffn_w_gu, ffn_w_down, moe_w_router, moe_w_gu, moe_w_down):
    lb = jnp.cumsum(jax.nn.softmax(hg_lower_bounds.astype(F32), axis=0), axis=0)
    lb1 = (lb[1] - lb[0]).reshape(1, D_MODEL)
    w_router = jnp.pad(moe_w_router[0].astype(F32), ((0, 0), (0, LANES - N_EXPERTS)))
    w_router_hi = w_router.astype(BF16)
    w_router_lo = (w_router - w_router_hi.astype(F32)).astype(BF16)
    w_router = jnp.concatenate([w_router_hi, w_router_hi, w_router_lo], axis=0)
    weights = (norm_mix.astype(F32), norm_ch.astype(F32), norm_final.astype(F32),
               att_w_qkv[0].astype(BF16), att_w_o[0].astype(BF16), att_rel_bias[0],
               hg_w_in[0].astype(BF16), hg_w_out[0].astype(BF16), hg_out_norm[0].astype(F32),
               ffn_w_gu[0].astype(BF16), ffn_w_down[0].astype(BF16),
               w_router, moe_w_gu[0].astype(BF16), moe_w_down[0])

    yp, kp, vp, sp = _stream(x_prompt, None, None, None, lb1, weights, 512, 128, 32)
    s0_t = jnp.swapaxes(state_hgrn[0].astype(F32), -1, -2)
    dec_tokens = x_sample.shape[0] * x_sample.shape[1]
    ys, ks, vs, ss = _stream(x_sample, cache_k[0], cache_v[0], s0_t, lb1, weights,
                             dec_tokens, x_sample.shape[1], x_sample.shape[1])
    return (yp, ys, kp[None], vp[None], ks[None], vs[None], sp[None], ss[None])
```

```python
import functools

import jax
import jax.numpy as jnp
import numpy as np
from jax import lax
from jax.experimental import pallas as pl
from jax.experimental.pallas import tpu as pltpu

F32 = jnp.float32
BF16 = jnp.bfloat16
I32 = jnp.int32

D_MODEL = 1024
ATT_HEADS = 16
ATT_HEAD_DIM = 64
CHUNK = 64
LEFT_CHUNKS = 8
WINDOW = LEFT_CHUNKS * CHUNK
REL_CLIP = 128
HG_HEADS = 8
HG_EXPAND = 128
N_EXPERTS = 8
RMS_EPS = 1e-6

LANES = 128
F32_SUBLANES = 8
BF16_SUBLANES = 16
MXU_COLS = 256
VMEM_LIMIT = 56 * 1024 * 1024

HEAD_PAIRS = ATT_HEADS // 2
Q_GROUP = 2 * WINDOW
Q_HALF = WINDOW // 2
K_SPAN = Q_HALF + WINDOW
PAIRS_PER_STEP = 4
MASKED = -1e30

ROUTE_TILE = 512
ROUTE_ALIGN = BF16_SUBLANES
ROUTE_ROWS = 2 * ROUTE_TILE + N_EXPERTS * (ROUTE_ALIGN - 1)
ROUTE_ROWS = -(-ROUTE_ROWS // LANES) * LANES
RUN_SIZES = tuple(ROUTE_TILE >> s for s in range(6))
EXPERT_TILE = 1024

SCAN_CHUNKS_PER_STEP = 8
ROUTER_PARTS = 4

_NT = (((1,), (1,)), ((), ()))
_TN = (((0,), (0,)), ((), ()))


def _params(*sem):
    return pltpu.CompilerParams(dimension_semantics=sem, vmem_limit_bytes=VMEM_LIMIT)


def _rms(x, gain):
    return x * lax.rsqrt(jnp.mean(x * x, axis=-1, keepdims=True) + RMS_EPS) * gain


def _sigmoid(x):
    return 1.0 / (1.0 + jnp.exp(-x))


def _const_spec(shape):
    zeros = (0,) * len(shape)
    return pl.BlockSpec(shape, lambda *_: zeros)


def _qkv_kernel(x_ref, g_ref, w_ref, q_ref, k_ref, v_ref, kf_ref, vf_ref, *, tiles_per_seq):
    h = _rms(x_ref[...], g_ref[...]).astype(BF16)
    d = D_MODEL
    q = jnp.dot(h, w_ref[:, 0:d], preferred_element_type=F32) * (ATT_HEAD_DIM ** -0.5)
    k = jnp.dot(h, w_ref[:, d:2 * d], preferred_element_type=F32)
    v = jnp.dot(h, w_ref[:, 2 * d:3 * d], preferred_element_type=F32)
    for p in range(HEAD_PAIRS):
        sl = slice(p * LANES, (p + 1) * LANES)
        q_ref[p] = q[:, sl].astype(BF16)
        k_ref[p] = k[:, sl].astype(BF16)
        v_ref[p] = v[:, sl].astype(BF16)

    @pl.when(pl.program_id(0) % tiles_per_seq == tiles_per_seq - 1)
    def _():
        kf_ref[...] = k.reshape(kf_ref.shape)
        vf_ref[...] = v.reshape(vf_ref.shape)


def _qkv_call(x, gain, w, tiles_per_seq, tm):
    n = x.shape[0]
    n_tiles = n // tm
    n_seq = n_tiles // tiles_per_seq
    slab = jax.ShapeDtypeStruct((HEAD_PAIRS, n, LANES), BF16)
    tail = jax.ShapeDtypeStruct((n_seq * tm, ATT_HEADS, ATT_HEAD_DIM), F32)
    slab_spec = pl.BlockSpec((HEAD_PAIRS, tm, LANES), lambda i: (0, i, 0))
    tail_spec = pl.BlockSpec((tm, ATT_HEADS, ATT_HEAD_DIM), lambda i: (i // tiles_per_seq, 0, 0))
    return pl.pallas_call(
        functools.partial(_qkv_kernel, tiles_per_seq=tiles_per_seq),
        grid=(n_tiles,),
        in_specs=[pl.BlockSpec((tm, D_MODEL), lambda i: (i, 0)),
                  _const_spec((1, D_MODEL)),
                  _const_spec((D_MODEL, 3 * D_MODEL))],
        out_specs=[slab_spec, slab_spec, slab_spec, tail_spec, tail_spec],
        out_shape=[slab, slab, slab, tail, tail],
        compiler_params=_params("arbitrary"),
        name="qkv_proj",
    )(x, gain, w)


def _band_attn_kernel(q_ref, kp_ref, kc_ref, vp_ref, vc_ref, bias_ref, start_ref, o_ref):
    first_group = pl.program_id(2) == 0
    lane = lax.broadcasted_iota(I32, (2 * Q_HALF, LANES), 1)
    row = lax.broadcasted_iota(I32, (2 * Q_HALF, LANES), 0)
    own = (lane < ATT_HEAD_DIM) == (row < Q_HALF)
    first_head = lax.broadcasted_iota(I32, (Q_HALF, LANES), 1) < ATT_HEAD_DIM
    def window(prev_ref, cur_ref, pr, lo):
        hi = lo + K_SPAN
        if lo >= WINDOW:
            return cur_ref[pr, lo - WINDOW:hi - WINDOW, :]
        return jnp.concatenate([prev_ref[pr, lo:, :], cur_ref[pr, 0:hi - WINDOW, :]], axis=0)

    for pr in range(PAIRS_PER_STEP):
        for part in range(q_ref.shape[1] // Q_HALF):
            rows = slice(part * Q_HALF, (part + 1) * Q_HALF)
            qh = q_ref[pr, rows, :]
            q2 = jnp.concatenate([qh, qh], axis=0)
            q2 = jnp.where(own, q2, jnp.zeros_like(q2))
            keys = window(kp_ref, kc_ref, pr, part * Q_HALF)
            vals = window(vp_ref, vc_ref, pr, part * Q_HALF)
            s = lax.dot_general(q2, keys, _NT, preferred_element_type=F32) + bias_ref[pr]
            if (part + 1) * Q_HALF <= WINDOW:
                s = s + jnp.where(first_group, start_ref[part:part + 1, :], 0.0)
            m = jnp.max(s, axis=-1, keepdims=True)
            p = jnp.exp(s - m)
            l = jnp.sum(p, axis=-1, keepdims=True)
            o2 = jnp.dot(p.astype(BF16), vals, preferred_element_type=F32) / l
            o = jnp.where(first_head, o2[:Q_HALF], o2[Q_HALF:])
            o_ref[rows, pr * LANES:(pr + 1) * LANES] = o.astype(BF16)


def _band_bias(table):
    qi = np.arange(Q_HALF)[:, None]
    kj = np.arange(K_SPAN)[None, :]
    cq, ck = qi // CHUNK, kj // CHUNK
    band = (ck >= cq) & (ck <= cq + LEFT_CHUNKS)
    start = np.where(np.concatenate([kj >= WINDOW, kj >= WINDOW - Q_HALF]), 0.0, MASKED).astype(np.float32)
    table = table.astype(F32)
    n_rel = table.shape[1]
    span = Q_HALF + K_SPAN - 1
    lo_rep = (K_SPAN - 1 - WINDOW) - (CHUNK - 1)
    hi_rep = span - lo_rep - n_rel
    line = jnp.concatenate([jnp.repeat(table[:, :1], lo_rep, axis=1), table,
                            jnp.repeat(table[:, -1:], hi_rep, axis=1)], axis=1)[:, ::-1]
    flow = jnp.tile(line, (1, Q_HALF + 1))[:, Q_HALF - 1:Q_HALF - 1 + Q_HALF * (span - 1)]
    b = flow.reshape(-1, Q_HALF, span - 1)[:, :, :K_SPAN]
    b = jnp.where(band[None], b, MASKED)
    return b.reshape(HEAD_PAIRS, 2 * Q_HALF, K_SPAN), jnp.asarray(start)


def _band_attn_call(q, k, v, bias, start, batch, seq):
    groups = seq // Q_GROUP
    n = batch * seq

    def cur(p, b, g):
        return (p, b * groups + g, 0)

    def prev(p, b, g):
        return (p, (b * groups + g) * (Q_GROUP // WINDOW) - jnp.minimum(g, 1), 0)

    blk = (PAIRS_PER_STEP, Q_GROUP, LANES)
    blk_prev = (PAIRS_PER_STEP, WINDOW, LANES)
    return pl.pallas_call(
        _band_attn_kernel,
        grid=(HEAD_PAIRS // PAIRS_PER_STEP, batch, groups),
        in_specs=[pl.BlockSpec(blk, cur), pl.BlockSpec(blk_prev, prev), pl.BlockSpec(blk, cur),
                  pl.BlockSpec(blk_prev, prev), pl.BlockSpec(blk, cur),
                  pl.BlockSpec((PAIRS_PER_STEP, 2 * Q_HALF, K_SPAN), lambda p, b, g: (p, 0, 0)),
                  _const_spec((2, K_SPAN))],
        out_specs=pl.BlockSpec((Q_GROUP, PAIRS_PER_STEP * LANES), lambda p, b, g: (b * groups + g, p)),
        out_shape=jax.ShapeDtypeStruct((n, D_MODEL), BF16),
        compiler_params=_params("arbitrary", "arbitrary", "arbitrary"),
        name="band_attention",
    )(q, k, k, v, v, bias, start)


def _cache_attn_kernel(q_ref, k_ref, v_ref, ck_ref, cv_ref, bias_ref, o_ref):
    s_len = q_ref.shape[1]
    lane = lax.broadcasted_iota(I32, (s_len, LANES), 1)
    low = lane < ATT_HEAD_DIM
    scores = []
    for p in range(HEAD_PAIRS):
        sl = slice(p * LANES, (p + 1) * LANES)
        keys = jnp.concatenate([ck_ref[:, sl].astype(BF16), k_ref[p]], axis=0)
        qp = q_ref[p]
        for hd in range(2):
            qm = jnp.where(low if hd == 0 else jnp.logical_not(low), qp, jnp.zeros_like(qp))
            scores.append(lax.dot_general(qm, keys, _NT, preferred_element_type=F32) + bias_ref[2 * p + hd])
    probs = []
    for s in scores:
        m = jnp.max(s, axis=-1, keepdims=True)
        e = jnp.exp(s - m)
        probs.append((e.astype(BF16), jnp.sum(e, axis=-1, keepdims=True)))
    for p in range(HEAD_PAIRS):
        sl = slice(p * LANES, (p + 1) * LANES)
        vals = jnp.concatenate([cv_ref[:, sl].astype(BF16), v_ref[p]], axis=0)
        outs = [jnp.dot(e, vals, preferred_element_type=F32) / l for e, l in probs[2 * p:2 * p + 2]]
        o_ref[:, sl] = jnp.where(low, outs[0], outs[1]).astype(BF16)


def _cache_bias(table, w, s_len):
    qi = np.arange(s_len)[:, None]
    kj = np.arange(w + s_len)[None, :]
    idx = np.clip(w + qi - kj, -(CHUNK - 1), REL_CLIP) + (CHUNK - 1)
    return table.astype(F32)[:, idx]


def _cache_attn_call(q, k, v, cache_k, cache_v, bias, batch, s_len):
    w = cache_k.shape[1]
    slab = pl.BlockSpec((HEAD_PAIRS, s_len, LANES), lambda b: (0, b, 0))
    cache = pl.BlockSpec((None, w, D_MODEL), lambda b: (b, 0, 0))
    return pl.pallas_call(
        _cache_attn_kernel,
        grid=(batch,),
        in_specs=[slab, slab, slab, cache, cache, _const_spec((ATT_HEADS, s_len, w + s_len))],
        out_specs=pl.BlockSpec((s_len, D_MODEL), lambda b: (b, 0)),
        out_shape=jax.ShapeDtypeStruct((batch * s_len, D_MODEL), BF16),
        compiler_params=_params("arbitrary"),
        name="cache_attention",
    )(q, k, v, cache_k, cache_v, bias)


def _swiglu_block(h, wg_ref, wu_ref, wd_ref, act_ref):
    fc = act_ref.shape[1]
    for c0 in range(0, fc, MXU_COLS):
        sl = slice(c0, min(c0 + MXU_COLS, fc))
        gate = jnp.dot(h, wg_ref[:, sl], preferred_element_type=F32)
        up = jnp.dot(h, wu_ref[:, sl], preferred_element_type=F32)
        act_ref[:, sl] = (gate * _sigmoid(gate) * up).astype(BF16)
    return jnp.dot(act_ref[...], wd_ref[...].astype(BF16), preferred_element_type=F32)


def _ffn_kernel(a_ref, wo_ref, x_ref, g_ref, wg_ref, wu_ref, wd_ref, o_ref, h_ref, act_ref):
    j = pl.program_id(1)

    @pl.when(j == 0)
    def _():
        x1 = x_ref[...] + jnp.dot(a_ref[...], wo_ref[...], preferred_element_type=F32)
        o_ref[...] = x1
        h_ref[...] = _rms(x1, g_ref[...]).astype(BF16)

    o_ref[...] += _swiglu_block(h_ref[...], wg_ref, wu_ref, wd_ref, act_ref)


def _ffn_call(a, w_o, x, gain, w_gu, w_down, tm, fc):
    n = x.shape[0]
    d_ff = w_down.shape[0]
    nf = d_ff // fc
    row = pl.BlockSpec((tm, D_MODEL), lambda i, j: (i, 0))
    return pl.pallas_call(
        _ffn_kernel,
        grid=(n // tm, nf),
        in_specs=[row, _const_spec((D_MODEL, D_MODEL)), row, _const_spec((1, D_MODEL)),
                  pl.BlockSpec((D_MODEL, fc), lambda i, j: (0, j)),
                  pl.BlockSpec((D_MODEL, fc), lambda i, j: (0, j + nf)),
                  pl.BlockSpec((fc, D_MODEL), lambda i, j: (j, 0))],
        out_specs=row,
        out_shape=jax.ShapeDtypeStruct((n, D_MODEL), F32),
        scratch_shapes=[pltpu.VMEM((tm, D_MODEL), BF16), pltpu.VMEM((tm, fc), BF16)],
        compiler_params=_params("arbitrary", "arbitrary"),
        name="swiglu_ffn",
    )(a, w_o, x, gain, w_gu, w_gu, w_down)


def _hg_in_kernel(x_ref, g_ref, w_ref, lb_ref, q_ref, f_ref, v_ref, z_ref):
    d = D_MODEL
    lb = lb_ref[...]
    half = x_ref.shape[0] // 2
    for p in (slice(0, half), slice(half, 2 * half)):
        h = _rms(x_ref[p, :], g_ref[...]).astype(BF16)

        def proj(idx):
            return jnp.dot(h, w_ref[:, idx * d:(idx + 1) * d], preferred_element_type=F32)

        f_ref[p, :] = lb + (1.0 - lb) * _sigmoid(proj(1))
        z = proj(3)
        z_ref[p, :] = z * _sigmoid(z)
        q_ref[p, :] = proj(0) * (HG_EXPAND ** -0.5)
        v_ref[p, :] = proj(2).astype(BF16)


def _hg_in_call(x, gain, w, lb, tm):
    n = x.shape[0]
    row = pl.BlockSpec((tm, D_MODEL), lambda i: (i, 0))
    out = jax.ShapeDtypeStruct((n, D_MODEL), F32)
    return pl.pallas_call(
        _hg_in_kernel,
        grid=(n // tm,),
        in_specs=[row, _const_spec((1, D_MODEL)), _const_spec((D_MODEL, 4 * D_MODEL)),
                  _const_spec((1, D_MODEL))],
        out_specs=[row, row, row, row],
        out_shape=[out, out, jax.ShapeDtypeStruct((n, D_MODEL), BF16), out],
        compiler_params=_params("arbitrary"),
        name="hgrn_in_proj",
    )(x, gain, w, lb)


def _hg_scan_kernel(*refs, chunk, sub, has_s0):
    st_refs = refs[-HG_HEADS:]
    refs = refs[:-HG_HEADS]
    if has_s0:
        q_ref, f_ref, v_ref, z_ref, on_ref, s0_ref, y_ref, sf_ref = refs
    else:
        q_ref, f_ref, v_ref, z_ref, on_ref, y_ref, sf_ref = refs
    t = pl.program_id(1)

    @pl.when(t == 0)
    def _():
        for hd in range(HG_HEADS):
            st_refs[hd][...] = s0_ref[hd] if has_s0 else jnp.zeros((HG_EXPAND, HG_EXPAND), F32)

    r = lax.broadcasted_iota(I32, (chunk, chunk), 0)
    c = lax.broadcasted_iota(I32, (chunk, chunk), 1)
    causal = c <= r
    tri = causal.astype(BF16)
    n_sub = chunk // sub
    for c0 in range(0, q_ref.shape[0], chunk):
        _hg_scan_chunk(slice(c0, c0 + chunk), q_ref, f_ref, v_ref, z_ref, on_ref, y_ref, st_refs,
                       r, causal, tri, chunk, sub, n_sub)

    @pl.when(t == pl.num_programs(1) - 1)
    def _():
        for hd in range(HG_HEADS):
            sf_ref[hd] = st_refs[hd][...]


def _hg_scan_chunk(rows, q_ref, f_ref, v_ref, z_ref, on_ref, y_ref, st_refs,
                   r, causal, tri, chunk, sub, n_sub):
    forget = f_ref[rows, :]
    log_f = jnp.log(forget)
    key_all = 1.0 - forget
    hi = log_f.astype(BF16)
    rest = log_f - hi.astype(F32)
    mid = rest.astype(BF16)
    lo = (rest - mid.astype(F32)).astype(BF16)
    g_all = jnp.dot(jnp.concatenate([tri, tri, tri], axis=1), jnp.concatenate([hi, mid, lo], axis=0),
                    preferred_element_type=F32)

    heads = [slice(hd * HG_EXPAND, (hd + 1) * HG_EXPAND) for hd in range(HG_HEADS)]
    values, inter, scores = [], [], []
    for hd, sl in enumerate(heads):
        g = g_all[:, sl]
        q = q_ref[rows, sl]
        k = key_all[:, sl]
        v = v_ref[rows, sl]
        g_last = g[chunk - 1:chunk, :]
        state = st_refs[hd][...]
        q_in = (q * jnp.exp(g)).astype(BF16)
        k_end = (k * jnp.exp(g_last - g)).astype(BF16)
        o_inter = lax.dot_general(q_in, state.astype(BF16), _NT, preferred_element_type=F32)
        st_refs[hd][...] = state * jnp.exp(g_last) + lax.dot_general(v, k_end, _TN, preferred_element_type=F32)

        g_mids = [g[b * sub + sub // 2 - 1:b * sub + sub // 2, :] for b in range(n_sub)]
        g_mid_rows = jnp.concatenate([jnp.broadcast_to(m, (sub, HG_EXPAND)) for m in g_mids], axis=0)
        q_sub = (q * jnp.exp(g - g_mid_rows)).astype(BF16)
        k_blocks = []
        scaled = None
        for b in range(n_sub):
            blk = slice(b * sub, (b + 1) * sub)
            fresh = k[blk] * jnp.exp(g_mids[b] - g[blk])
            if b == 0:
                scaled = fresh
            else:
                scaled = jnp.concatenate([scaled * jnp.exp(g_mids[b] - g_mids[b - 1]), fresh], axis=0)
            k_blocks.append(scaled.astype(BF16))
            if (b + 1) * sub < chunk:
                k_blocks.append(jnp.zeros((chunk - (b + 1) * sub, HG_EXPAND), BF16))
        k_sub = k_blocks[0] if len(k_blocks) == 1 else jnp.concatenate(k_blocks, axis=0)
        values.append(v)
        inter.append(o_inter)
        scores.append(lax.dot_general(q_sub, k_sub, _NT, preferred_element_type=F32))

    outs = []
    for v, o_inter, a_all in zip(values, inter, scores):
        a = a_all[:, 0:chunk]
        for b in range(1, n_sub):
            a = jnp.where(r >= b * sub, a_all[:, b * chunk:(b + 1) * chunk], a)
        a = jnp.where(causal, a, 0.0).astype(BF16)
        outs.append(o_inter + jnp.dot(a, v, preferred_element_type=F32))
    for sl, o in zip(heads, outs):
        o = o * lax.rsqrt(jnp.mean(o * o, axis=-1, keepdims=True) + RMS_EPS) * on_ref[:, sl]
        y_ref[rows, sl] = (o * z_ref[rows, sl]).astype(BF16)


def _hg_scan_call(q, f, v, z, out_norm, s0_t, batch, seq, chunk, sub):
    step_rows = min(seq, SCAN_CHUNKS_PER_STEP * chunk)
    n_chunks = seq // step_rows
    has_s0 = s0_t is not None
    row = pl.BlockSpec((step_rows, D_MODEL), lambda b, t: (b * n_chunks + t, 0))
    st = pl.BlockSpec((None, HG_HEADS, HG_EXPAND, HG_EXPAND), lambda b, t: (b, 0, 0, 0))
    in_specs = [row, row, row, row, _const_spec((1, D_MODEL))]
    args = [q, f, v, z, out_norm]
    if has_s0:
        in_specs.append(st)
        args.append(s0_t)
    return pl.pallas_call(
        functools.partial(_hg_scan_kernel, chunk=chunk, sub=sub, has_s0=has_s0),
        grid=(batch, n_chunks),
        in_specs=in_specs,
        out_specs=[row, st],
        out_shape=[jax.ShapeDtypeStruct((batch * seq, D_MODEL), BF16),
                   jax.ShapeDtypeStruct((batch, HG_HEADS, HG_EXPAND, HG_EXPAND), F32)],
        scratch_shapes=[pltpu.VMEM((HG_EXPAND, HG_EXPAND), F32)] * HG_HEADS,
        compiler_params=_params("arbitrary", "arbitrary"),
        name="hgrn_scan",
    )(*args)


def _router_kernel(y_ref, wo_ref, x_ref, g_ref, wr_ref, xo_ref, h_ref, meta_ref, cnt_ref):
    tr = x_ref.shape[0]
    parts = [slice(p * (tr // ROUTER_PARTS), (p + 1) * (tr // ROUTER_PARTS)) for p in range(ROUTER_PARTS)]
    xs = [x_ref[p, :] + jnp.dot(y_ref[p, :], wo_ref[...], preferred_element_type=F32) for p in parts]
    for p, x in zip(parts, xs):
        xo_ref[p, :] = x
    hfs = [_rms(x, g_ref[...]) for x in xs]
    logit_parts = []
    for p, hf in zip(parts, hfs):
        hi = hf.astype(BF16)
        lo = (hf - hi.astype(F32)).astype(BF16)
        h_ref[p, :] = hi
        logit_parts.append(jnp.dot(jnp.concatenate([hi, lo, hi], axis=1), wr_ref[...],
                                   preferred_element_type=F32))
    logits = jnp.concatenate(logit_parts, axis=0)
    lane = lax.broadcasted_iota(I32, (tr, LANES), 1).astype(F32)
    lg = jnp.where(lane < N_EXPERTS, logits, -jnp.inf)
    m1 = jnp.max(lg, axis=-1, keepdims=True)
    i1 = jnp.min(jnp.where(lg == m1, lane, float(LANES)), axis=-1, keepdims=True)
    lg2 = jnp.where(lane == i1, -jnp.inf, lg)
    m2 = jnp.max(lg2, axis=-1, keepdims=True)
    i2 = jnp.min(jnp.where(lg2 == m2, lane, float(LANES)), axis=-1, keepdims=True)
    e = jnp.exp(m2 - m1)
    gate1 = 1.0 / (1.0 + e)
    gate2 = e / (1.0 + e)

    chosen = jnp.logical_or(lane == i1, lane == i2)
    r = lax.broadcasted_iota(I32, (tr, tr), 0)
    c = lax.broadcasted_iota(I32, (tr, tr), 1)
    before = (c < r).astype(BF16)
    rank = jnp.dot(before, chosen.astype(BF16), preferred_element_type=F32)
    counts = jnp.sum(chosen.astype(F32), axis=0, keepdims=True)
    padded = jnp.floor((counts + (ROUTE_ALIGN - 1)) * (1.0 / ROUTE_ALIGN)) * ROUTE_ALIGN
    r2 = lax.broadcasted_iota(I32, (LANES, LANES), 0)
    c2 = lax.broadcasted_iota(I32, (LANES, LANES), 1)
    run_start = jnp.dot(jnp.broadcast_to(padded, (F32_SUBLANES, LANES)).astype(BF16), (r2 < c2).astype(BF16),
                        preferred_element_type=F32)[0:1]
    slot = run_start + rank
    slot1 = jnp.sum(jnp.where(lane == i1, slot, 0.0), axis=-1, keepdims=True)
    slot2 = jnp.sum(jnp.where(lane == i2, slot, 0.0), axis=-1, keepdims=True)
    meta = jnp.where(lane == 0, slot1, 0.0)
    for idx, val in ((1, slot2), (2, gate1), (3, gate2), (4, i1), (5, i2)):
        meta = jnp.where(lane == idx, val, meta)
    meta_ref[...] = meta
    cnt_ref[...] = jnp.broadcast_to(counts.astype(I32)[None], cnt_ref.shape)


def _router_call(y, w_out, x, gain, w_router_stack):
    n = x.shape[0]
    n_tiles = n // ROUTE_TILE
    row = pl.BlockSpec((ROUTE_TILE, D_MODEL), lambda i: (i, 0))
    return pl.pallas_call(
        _router_kernel,
        grid=(n_tiles,),
        in_specs=[row, _const_spec((D_MODEL, D_MODEL)), row, _const_spec((1, D_MODEL)),
                  _const_spec((3 * D_MODEL, LANES))],
        out_specs=[row, row, pl.BlockSpec((ROUTE_TILE, LANES), lambda i: (i, 0)),
                   pl.BlockSpec((1, F32_SUBLANES, LANES), lambda i: (i, 0, 0))],
        out_shape=[jax.ShapeDtypeStruct((n, D_MODEL), F32),
                   jax.ShapeDtypeStruct((n, D_MODEL), BF16),
                   jax.ShapeDtypeStruct((n, LANES), F32),
                   jax.ShapeDtypeStruct((n_tiles, F32_SUBLANES, LANES), I32)],
        compiler_params=_params("arbitrary"),
        name="moe_router",
    )(y, w_out, x, gain, w_router_stack)


def _run_copies(tbl_ref, tile, stage_ref, hbm_ref, sem, to_hbm):
    base = tile * (3 * N_EXPERTS)
    for e in range(N_EXPERTS):
        lo = tbl_ref[base + e]
        n = tbl_ref[base + N_EXPERTS + e]
        dst = tbl_ref[base + 2 * N_EXPERTS + e]
        for sz in RUN_SIZES:
            done = jnp.bitwise_and(n, ~(2 * sz - 1))
            st = stage_ref.at[pl.ds(pl.multiple_of(lo + done, ROUTE_ALIGN), sz)]
            hb = hbm_ref.at[pl.ds(pl.multiple_of(dst + done, ROUTE_ALIGN), sz)]
            copy = pltpu.make_async_copy(st, hb, sem) if to_hbm else pltpu.make_async_copy(hb, st, sem)
            yield jnp.bitwise_and(n, sz) != 0, copy


def _start_runs(tbl_ref, tile, stage_ref, hbm_ref, sem, to_hbm):
    for cond, copy in _run_copies(tbl_ref, tile, stage_ref, hbm_ref, sem, to_hbm):
        pl.when(cond)(copy.start)


def _wait_runs(tbl_ref, tile, stage_ref, hbm_ref, sem, to_hbm):
    for cond, copy in _run_copies(tbl_ref, tile, stage_ref, hbm_ref, sem, to_hbm):
        pl.when(cond)(copy.wait)


def _dispatch_kernel(tbl_ref, h_ref, meta_ref, xs_ref, stage_ref, sem):
    tile = pl.program_id(0)
    buf = tile % 2
    meta = meta_ref[...]
    slot1 = meta[:, 0:1].astype(I32)
    slot2 = meta[:, 1:2].astype(I32)
    rows = lax.broadcasted_iota(I32, (ROUTE_TILE, ROUTE_ROWS), 1)
    pick = jnp.logical_or(rows == slot1, rows == slot2).astype(BF16)
    stage_ref[buf] = lax.dot_general(pick, h_ref[...], _TN, preferred_element_type=F32).astype(BF16)
    _start_runs(tbl_ref, tile, stage_ref.at[buf], xs_ref, sem.at[buf], True)

    @pl.when(tile > 0)
    def _():
        _wait_runs(tbl_ref, tile - 1, stage_ref.at[1 - buf], xs_ref, sem.at[1 - buf], True)

    @pl.when(tile == pl.num_programs(0) - 1)
    def _():
        _wait_runs(tbl_ref, tile, stage_ref.at[buf], xs_ref, sem.at[buf], True)


def _dispatch_call(tbl, h, meta, sorted_rows):
    n = h.shape[0]
    grid_spec = pltpu.PrefetchScalarGridSpec(
        num_scalar_prefetch=1,
        grid=(n // ROUTE_TILE,),
        in_specs=[pl.BlockSpec((ROUTE_TILE, D_MODEL), lambda i, tbl: (i, 0)),
                  pl.BlockSpec((ROUTE_TILE, LANES), lambda i, tbl: (i, 0))],
        out_specs=pl.BlockSpec(memory_space=pl.ANY),
        scratch_shapes=[pltpu.VMEM((2, ROUTE_ROWS, D_MODEL), BF16), pltpu.SemaphoreType.DMA((2,))],
    )
    return pl.pallas_call(
        _dispatch_kernel,
        grid_spec=grid_spec,
        out_shape=jax.ShapeDtypeStruct((sorted_rows, D_MODEL), BF16),
        compiler_params=_params("arbitrary"),
        name="moe_dispatch",
    )(tbl, h, meta)


def _expert_kernel(te_ref, trow_ref, nval_ref, x_ref, wg_ref, wu_ref, wd_ref, o_ref, act_ref, acc_ref):
    i = pl.program_id(0)
    j = pl.program_id(1)
    nv = nval_ref[i]

    @pl.when(nv > 0)
    def _():
        @pl.when(j == 0)
        def _():
            acc_ref[...] = jnp.zeros_like(acc_ref)

        x = x_ref[...]
        row = lax.broadcasted_iota(I32, x.shape, 0)
        x = jnp.where(row < nv, x, jnp.zeros_like(x))
        acc_ref[...] += _swiglu_block(x, wg_ref, wu_ref, wd_ref, act_ref)

        @pl.when(j == pl.num_programs(1) - 1)
        def _():
            o_ref[...] = acc_ref[...].astype(BF16)


def _expert_call(te, trow, nval, xs, w_gu, w_down, fc, tile_rows):
    n_tiles = te.shape[0]
    d_ff = w_down.shape[1]
    nf = d_ff // fc

    def jj(i, j, nval):
        return jnp.where(nval[i] > 0, j, nf - 1)

    grid_spec = pltpu.PrefetchScalarGridSpec(
        num_scalar_prefetch=3,
        grid=(n_tiles, nf),
        in_specs=[pl.BlockSpec((tile_rows, D_MODEL), lambda i, j, te, trow, nval: (trow[i], 0)),
                  pl.BlockSpec((None, D_MODEL, fc), lambda i, j, te, trow, nval: (te[i], 0, jj(i, j, nval))),
                  pl.BlockSpec((None, D_MODEL, fc),
                               lambda i, j, te, trow, nval: (te[i], 0, jj(i, j, nval) + nf)),
                  pl.BlockSpec((None, fc, D_MODEL), lambda i, j, te, trow, nval: (te[i], jj(i, j, nval), 0))],
        out_specs=pl.BlockSpec((tile_rows, D_MODEL), lambda i, j, te, trow, nval: (trow[i], 0)),
        scratch_shapes=[pltpu.VMEM((tile_rows, fc), BF16), pltpu.VMEM((tile_rows, D_MODEL), F32)],
    )
    return pl.pallas_call(
        _expert_kernel,
        grid_spec=grid_spec,
        out_shape=jax.ShapeDtypeStruct(xs.shape, BF16),
        compiler_params=_params("arbitrary", "arbitrary"),
        name="moe_experts",
    )(te, trow, nval, xs, w_gu, w_gu, w_down)


def _combine_kernel(tbl_ref, ys_ref, meta_ref, x_ref, g_ref, o_ref, stage_ref, sem):
    tile = pl.program_id(0)
    buf = tile % 2

    @pl.when(tile == 0)
    def _():
        stage_ref[...] = jnp.zeros_like(stage_ref)
        _start_runs(tbl_ref, tile, stage_ref.at[buf], ys_ref, sem.at[buf], False)

    @pl.when(tile + 1 < pl.num_programs(0))
    def _():
        _start_runs(tbl_ref, tile + 1, stage_ref.at[1 - buf], ys_ref, sem.at[1 - buf], False)

    _wait_runs(tbl_ref, tile, stage_ref.at[buf], ys_ref, sem.at[buf], False)
    stage = stage_ref[buf]
    half = ROUTE_TILE // 2
    rows = lax.broadcasted_iota(I32, (half, ROUTE_ROWS), 1)
    picked = []
    for p in (slice(0, half), slice(half, ROUTE_TILE)):
        meta = meta_ref[p, :]
        y1 = jnp.dot((rows == meta[:, 0:1].astype(I32)).astype(BF16), stage, preferred_element_type=F32)
        y2 = jnp.dot((rows == meta[:, 1:2].astype(I32)).astype(BF16), stage, preferred_element_type=F32)
        picked.append((p, meta, y1, y2))
    for p, meta, y1, y2 in picked:
        x = x_ref[p, :] + (meta[:, 2:3] * y1 + meta[:, 3:4] * y2)
        o_ref[p, :] = _rms(x, g_ref[...])


def _combine_call(tbl, ys, meta, x, gain):
    n = x.shape[0]
    row = pl.BlockSpec((ROUTE_TILE, D_MODEL), lambda i, tbl: (i, 0))
    grid_spec = pltpu.PrefetchScalarGridSpec(
        num_scalar_prefetch=1,
        grid=(n // ROUTE_TILE,),
        in_specs=[pl.BlockSpec(memory_space=pl.ANY),
                  pl.BlockSpec((ROUTE_TILE, LANES), lambda i, tbl: (i, 0)),
                  row,
                  pl.BlockSpec((1, D_MODEL), lambda i, tbl: (0, 0))],
        out_specs=row,
        scratch_shapes=[pltpu.VMEM((2, ROUTE_ROWS, D_MODEL), BF16), pltpu.SemaphoreType.DMA((2,))],
    )
    return pl.pallas_call(
        _combine_kernel,
        grid_spec=grid_spec,
        out_shape=jax.ShapeDtypeStruct((n, D_MODEL), F32),
        compiler_params=_params("arbitrary"),
        name="moe_combine",
    )(tbl, ys, meta, x, gain)


def _routing_tables(counts, n_row_tiles, tile_rows):
    padded = (counts + (ROUTE_ALIGN - 1)) // ROUTE_ALIGN * ROUTE_ALIGN
    stage_off = jnp.cumsum(padded, axis=1) - padded
    total = jnp.sum(padded, axis=0)
    tiles_e = (total + (tile_rows - 1)) // tile_rows
    region = jnp.cumsum(tiles_e) - tiles_e
    dst = region[None, :] * tile_rows + jnp.cumsum(padded, axis=0) - padded
    tbl = jnp.concatenate([stage_off, padded, dst], axis=1).reshape(-1).astype(I32)

    used = jnp.sum(tiles_e)
    ids = jnp.arange(n_row_tiles, dtype=I32)
    ends = jnp.cumsum(tiles_e)
    te = jnp.sum((ids[:, None] >= ends[None, :]).astype(I32), axis=1)
    te = jnp.minimum(te, N_EXPERTS - 1)
    nval = jnp.clip(total[te] - (ids - region[te]) * tile_rows, 0, tile_rows)
    valid = ids < used
    nval = jnp.where(valid, nval, 0).astype(I32)
    last = jnp.maximum(used - 1, 0)
    te = jnp.where(valid, te, te[last]).astype(I32)
    trow = jnp.where(valid, ids, n_row_tiles).astype(I32)
    return tbl, te, trow, nval


def _moe_final(y, w_out, x, gain, w_router_stack, w_gu, w_down, final_gain, fc):
    n = x.shape[0]
    n_tiles = n // ROUTE_TILE
    tile_rows = min(EXPERT_TILE, -(-2 * n // N_EXPERTS // MXU_COLS) * MXU_COLS)
    max_rows = 2 * n + n_tiles * N_EXPERTS * (ROUTE_ALIGN - 1)
    n_row_tiles = -(-max_rows // tile_rows) + N_EXPERTS
    x, h, meta, cnt = _router_call(y, w_out, x, gain, w_router_stack)
    tbl, te, trow, nval = _routing_tables(cnt[:, 0, :N_EXPERTS], n_row_tiles, tile_rows)
    xs = _dispatch_call(tbl, h, meta, (n_row_tiles + 1) * tile_rows)
    ys = _expert_call(te, trow, nval, xs, w_gu, w_down, fc, tile_rows)
    return _combine_call(tbl, ys, meta, x, final_gain)


def _stream(x, cache_k, cache_v, s0_t, lb1, weights, tm, chunk, sub):
    (norm_mix, norm_ch, norm_final, w_qkv, w_o, rel_table, w_in, w_out, out_norm,
     w_gu, w_down, w_router, w_egu, w_edown) = weights
    batch, seq, _ = x.shape
    n = batch * seq
    x = x.reshape(n, D_MODEL)
    row = lambda a: a.reshape(1, D_MODEL)

    if cache_k is None:
        q, k, v, kf, vf = _qkv_call(x, row(norm_mix[0]), w_qkv, seq // tm, tm)
        att = _band_attn_call(q, k, v, *_band_bias(rel_table), batch, seq)
        keep = min(WINDOW, seq)
        new_k = kf.reshape(batch, keep, ATT_HEADS, ATT_HEAD_DIM)
        new_v = vf.reshape(batch, keep, ATT_HEADS, ATT_HEAD_DIM)
    else:
        q, k, v, kf, vf = _qkv_call(x, row(norm_mix[0]), w_qkv, 1, tm)
        w = cache_k.shape[1]
        att = _cache_attn_call(q, k, v, cache_k.reshape(batch, w, D_MODEL), cache_v.reshape(batch, w, D_MODEL),
                               _cache_bias(rel_table, w, seq), batch, seq)
        new_k = kf.reshape(batch, seq, ATT_HEADS, ATT_HEAD_DIM)
        new_v = vf.reshape(batch, seq, ATT_HEADS, ATT_HEAD_DIM)
    x = _ffn_call(att, w_o, x, row(norm_ch[0]), w_gu, w_down, min(n, 2 * tm), w_down.shape[0] // 2)

    q, f, v, z = _hg_in_call(x, row(norm_mix[1]), w_in, lb1, tm)
    y, s_t = _hg_scan_call(q, f, v, z, row(out_norm), s0_t, batch, seq, chunk, sub)
    y = _moe_final(y, w_out, x, row(norm_ch[1]), w_router, w_egu, w_edown, row(norm_final),
                   w_edown.shape[1] // 2)
    return y.reshape(batch, seq, D_MODEL), new_k, new_v, jnp.swapaxes(s_t, -1, -2)


def kernel(x_prompt, x_sample, cache_k, cache_v, state_hgrn, norm_mix, norm_ch, norm_final, att_w_qkv, att_w_o, att_rel_bias, hg_w_in, hg_w_out, hg_out_norm, hg_lower_bounds, ffn_w_gu, ffn_w_down, moe_w_router, moe_w_gu, moe_w_down):
    lb = jnp.cumsum(jax.nn.softmax(hg_lower_bounds.astype(F32), axis=0), axis=0)
    lb1 = (lb[1] - lb[0]).reshape(1, D_MODEL)
    w_router = jnp.pad(moe_w_router[0].astype(F32), ((0, 0), (0, LANES - N_EXPERTS)))
    w_router_hi = w_router.astype(BF16)
    w_router_lo = (w_router - w_router_hi.astype(F32)).astype(BF16)
    w_router = jnp.concatenate([w_router_hi, w_router_hi, w_router_lo], axis=0)
    weights = (norm_mix.astype(F32), norm_ch.astype(F32), norm_final.astype(F32),
               att_w_qkv[0].astype(BF16), att_w_o[0].astype(BF16), att_rel_bias[0],
               hg_w_in[0].astype(BF16), hg_w_out[0].astype(BF16), hg_out_norm[0].astype(F32),
               ffn_w_gu[0].astype(BF16), ffn_w_down[0].astype(BF16),
               w_router, moe_w_gu[0].astype(BF16), moe_w_down[0])

    yp, kp, vp, sp = _stream(x_prompt, None, None, None, lb1, weights, 512, 128, 32)
    s0_t = jnp.swapaxes(state_hgrn[0].astype(F32), -1, -2)
    dec_tokens = x_sample.shape[0] * x_sample.shape[1]
    ys, ks, vs, ss = _stream(x_sample, cache_k[0], cache_v[0], s0_t, lb1, weights,
                             dec_tokens, x_sample.shape[1], x_sample.shape[1])
    return (yp, ys, kp[None], vp[None], ks[None], vs[None], sp[None], ss[None])
```
